```python
import math
import jax
import jax.numpy as jnp
from jax import lax
import numpy as np

D_MODEL = 1024
BATCH = 8
SEQ = 4096
DEPTH = 4
DEC_BATCH = 1
DEC_SEQ = 16384
PAST_LEN = 128

GRID_W = 64
CHUNK = 128
EPS = 1e-6
ROPE_THETA = 10000.0

SSD_HEADS = 8
SSD_HEAD_DIM = 64
SSD_WIDTH = SSD_HEADS * SSD_HEAD_DIM
SSD_GROUPS = 2
SSD_STATE = 128
SSD_CONV = 5
SSD_XBC = SSD_WIDTH + 2 * SSD_GROUPS * SSD_STATE

RET_HEADS = 4
RET_QK_DIM = 128
RET_V_DIM = 128
RET_WIDTH = RET_HEADS * RET_V_DIM

ATT_HEADS = 8
ATT_KV_HEADS = 2
ATT_HEAD_DIM = 64
ATT_WIDTH = ATT_HEADS * ATT_HEAD_DIM

FNET_GROUPS = 4
FNET_GROUP_DIM = 128
FNET_WIDTH = FNET_GROUPS * FNET_GROUP_DIM

N_BRANCHES = 4
BRANCH_WIDTH = 512

IN_SIZES = (SSD_WIDTH, SSD_XBC, SSD_HEADS, SSD_HEADS,
            RET_HEADS * RET_QK_DIM, RET_HEADS * RET_QK_DIM, RET_WIDTH, RET_WIDTH,
            ATT_WIDTH, ATT_KV_HEADS * ATT_HEAD_DIM, ATT_KV_HEADS * ATT_HEAD_DIM,
            FNET_WIDTH, N_BRANCHES * D_MODEL)
D_IN = sum(IN_SIZES)

N_EXPERTS = 16
EXPERT_FF = 1024
EC_CAPACITY_FACTOR = 2

kernel_name = "hybrid_bidir_encoder_ssd_ret_gqa_fnet_ec"


def rms_norm(x, w):
    xf = x.astype(jnp.float32)
    y = xf * lax.rsqrt(jnp.mean(xf * xf, axis=-1, keepdims=True) + EPS)
    return (y * w.astype(jnp.float32)).astype(x.dtype)


def axial_angles(rows, head_dim):
    n_freq = head_dim // 4
    freqs = ROPE_THETA ** (-jnp.arange(n_freq, dtype=jnp.float32) / n_freq)
    t_row = jnp.repeat(jnp.arange(rows, dtype=jnp.float32), GRID_W)
    t_col = jnp.tile(jnp.arange(GRID_W, dtype=jnp.float32), rows)
    return t_row[:, None] * freqs, t_col[:, None] * freqs


def rotate(x, ang):
    m = ang.shape[-1]
    cos = jnp.cos(ang)[:, None, :]
    sin = jnp.sin(ang)[:, None, :]
    x1, x2 = x[..., :m], x[..., m:]
    return jnp.concatenate([x1 * cos - x2 * sin, x1 * sin + x2 * cos], axis=-1)


def axial_rotary(x, ang_row, ang_col):
    half = x.shape[-1] // 2
    xf = x.astype(jnp.float32)
    return jnp.concatenate([rotate(xf[..., :half], ang_row), rotate(xf[..., half:], ang_col)], axis=-1)


def centred_depthwise_conv(x, w, b):
    pad = SSD_CONV // 2
    y = lax.conv_general_dilated(x, w[:, None, :].astype(x.dtype), window_strides=(1,),
                                 padding=[(pad, pad)], dimension_numbers=("NWC", "WIO", "NWC"),
                                 feature_group_count=x.shape[-1])
    return y + b.astype(x.dtype)


def ssd_scan(x, dt, a, b_mat, c_mat, strict):
    bsz, seq = x.shape[:2]
    nc = seq // CHUNK
    r = SSD_HEADS // SSD_GROUPS
    xc = x.astype(jnp.float32).reshape(bsz, nc, CHUNK, SSD_GROUPS, r, SSD_HEAD_DIM)
    dtc = dt.reshape(bsz, nc, CHUNK, SSD_GROUPS, r)
    bc = b_mat.astype(jnp.float32).reshape(bsz, nc, CHUNK, SSD_GROUPS, SSD_STATE)
    cc = c_mat.astype(jnp.float32).reshape(bsz, nc, CHUNK, SSD_GROUPS, SSD_STATE)
    a_cs = jnp.cumsum(dtc * a.reshape(SSD_GROUPS, r), axis=2)
    xdt = xc * dtc[..., None]
    mask = jnp.tril(jnp.ones((CHUNK, CHUNK), dtype=bool), k=-1 if strict else 0)
    seg = a_cs[:, :, :, None] - a_cs[:, :, None, :]
    decay = jnp.exp(jnp.where(mask[:, :, None, None], seg, -jnp.inf))
    scores = jnp.einsum("bclgn,bcsgn->bclsg", cc, bc)
    y_diag = jnp.einsum("bclsg,bclsgr,bcsgrp->bclgrp", scores, decay, xdt)
    decay_to_end = jnp.exp(a_cs[:, :, -1:] - a_cs)
    states = jnp.einsum("bclgn,bclgr,bclgrp->bcgrpn", bc, decay_to_end, xdt)
    chunk_decay = jnp.exp(a_cs[:, :, -1])

    def step(h, inp):
        st, dec = inp
        return h * dec[..., None, None] + st, h

    h0 = jnp.zeros((bsz, SSD_GROUPS, r, SSD_HEAD_DIM, SSD_STATE), jnp.float32)
    _, h_in = lax.scan(step, h0, (jnp.moveaxis(states, 1, 0), jnp.moveaxis(chunk_decay, 1, 0)))
    h_in = jnp.moveaxis(h_in, 0, 1)
    y_off = jnp.einsum("bclgn,bcgrpn,bclgr->bclgrp", cc, h_in, jnp.exp(a_cs))
    return (y_diag + y_off).reshape(bsz, seq, SSD_HEADS, SSD_HEAD_DIM)


def ssd_mixer(z, xbc, dt_f, dt_b, conv_w, conv_b, dt_bias, a_log, d_skip, norm_w):
    bsz, seq, _ = z.shape
    xbc = jax.nn.silu(centred_depthwise_conv(xbc, conv_w, conv_b))
    gn = SSD_GROUPS * SSD_STATE
    x = xbc[..., :SSD_WIDTH].reshape(bsz, seq, SSD_HEADS, SSD_HEAD_DIM)
    b_mat = xbc[..., SSD_WIDTH:SSD_WIDTH + gn].reshape(bsz, seq, SSD_GROUPS, SSD_STATE)
    c_mat = xbc[..., SSD_WIDTH + gn:].reshape(bsz, seq, SSD_GROUPS, SSD_STATE)
    a = -jnp.exp(a_log.astype(jnp.float32))
    dtb = dt_bias.astype(jnp.float32)
    dtf = jax.nn.softplus(dt_f.astype(jnp.float32) + dtb[0])
    dtr = jax.nn.softplus(dt_b.astype(jnp.float32) + dtb[1])
    y_f = ssd_scan(x, dtf, a[0], b_mat, c_mat, strict=False)
    y_b = ssd_scan(jnp.flip(x, 1), jnp.flip(dtr, 1), a[1], jnp.flip(b_mat, 1), jnp.flip(c_mat, 1), strict=True)
    y = y_f + jnp.flip(y_b, 1) + x.astype(jnp.float32) * d_skip.astype(jnp.float32)[:, None]
    y = y.reshape(bsz, seq, SSD_WIDTH) * jax.nn.silu(z.astype(jnp.float32))
    return rms_norm(y, norm_w).astype(z.dtype)


def retention_scan(q, k, v, log_gamma, strict):
    bsz, seq = q.shape[:2]
    nc = seq // CHUNK
    qc = q.reshape(bsz, nc, CHUNK, RET_HEADS, RET_QK_DIM)
    kc = k.reshape(bsz, nc, CHUNK, RET_HEADS, RET_QK_DIM)
    vc = v.reshape(bsz, nc, CHUNK, RET_HEADS, RET_V_DIM)
    pos = jnp.arange(CHUNK, dtype=jnp.float32)
    mask = jnp.tril(jnp.ones((CHUNK, CHUNK), dtype=bool), k=-1 if strict else 0)
    diff = pos[:, None] - pos[None, :]
    decay = jnp.exp(jnp.where(mask[:, :, None], diff[:, :, None] * log_gamma, -jnp.inf))
    scores = jnp.einsum("bclhd,bcshd->bchls", qc, kc) * jnp.moveaxis(decay, 2, 0)
    inner = jnp.einsum("bchls,bcshv->bclhv", scores, vc)
    k_decay = jnp.exp((CHUNK - 1 - pos)[:, None] * log_gamma)
    states = jnp.einsum("bcshd,sh,bcshv->bchdv", kc, k_decay, vc)
    chunk_decay = jnp.exp(CHUNK * log_gamma)

    def step(h, st):
        return h * chunk_decay[:, None, None] + st, h

    h0 = jnp.zeros((bsz, RET_HEADS, RET_QK_DIM, RET_V_DIM), jnp.float32)
    _, h_in = lax.scan(step, h0, jnp.moveaxis(states, 1, 0))
    h_in = jnp.moveaxis(h_in, 0, 1)
    q_decay = jnp.exp((pos + 1.0)[:, None] * log_gamma)
    cross = jnp.einsum("bclhd,lh,bchdv->bclhv", qc, q_decay, h_in)
    return (inner + cross).reshape(bsz, seq, RET_HEADS, RET_V_DIM)


def retention_mixer(q, k, v, g, log2_decay, ang_row, ang_col):
    bsz, seq, _ = q.shape
    q = axial_rotary(q.reshape(bsz, seq, RET_HEADS, RET_QK_DIM), ang_row, ang_col)
    k = axial_rotary(k.reshape(bsz, seq, RET_HEADS, RET_QK_DIM), ang_row, ang_col) * (RET_QK_DIM ** -0.5)
    v = v.astype(jnp.float32).reshape(bsz, seq, RET_HEADS, RET_V_DIM)
    log_gamma = jnp.log1p(-jnp.exp2(log2_decay.astype(jnp.float32)))
    y_f = retention_scan(q, k, v, log_gamma[0], strict=False)
    y_b = retention_scan(jnp.flip(q, 1), jnp.flip(k, 1), jnp.flip(v, 1), log_gamma[1], strict=True)
    y = y_f + jnp.flip(y_b, 1)
    y = y * lax.rsqrt(jnp.mean(y * y, axis=-1, keepdims=True) + EPS)
    y = y.reshape(bsz, seq, RET_WIDTH) * jax.nn.silu(g.astype(jnp.float32))
    return y.astype(g.dtype)


def attention_mixer(q, k, v, q_norm, k_norm, ang_row, ang_col):
    bsz, seq, _ = q.shape
    dtype = q.dtype
    r = ATT_HEADS // ATT_KV_HEADS
    q = rms_norm(q.reshape(bsz, seq, ATT_HEADS, ATT_HEAD_DIM), q_norm)
    k = rms_norm(k.reshape(bsz, seq, ATT_KV_HEADS, ATT_HEAD_DIM), k_norm)
    q = axial_rotary(q, ang_row, ang_col).astype(dtype)
    k = axial_rotary(k, ang_row, ang_col).astype(dtype)
    v = v.reshape(bsz, seq, ATT_KV_HEADS, ATT_HEAD_DIM)
    nb = seq // CHUNK
    qb = jnp.moveaxis(q.reshape(bsz, nb, CHUNK, ATT_KV_HEADS, r, ATT_HEAD_DIM), 1, 0)
    scale = ATT_HEAD_DIM ** -0.5

    def attend_block(q_blk):
        s = jnp.einsum("bqgrd,bkgd->bgrqk", q_blk, k, preferred_element_type=jnp.float32) * scale
        p = jax.nn.softmax(s, axis=-1)
        return jnp.einsum("bgrqk,bkgd->bqgrd", p.astype(v.dtype), v)

    o = lax.map(attend_block, qb)
    return jnp.moveaxis(o, 0, 1).reshape(bsz, seq, ATT_WIDTH)


def fourier_mixer(f):
    bsz, seq, _ = f.shape
    fg = f.astype(jnp.float32).reshape(bsz, seq, FNET_GROUPS, FNET_GROUP_DIM)
    out = jnp.fft.fft2(fg, axes=(1, 3), norm="ortho").real
    return out.reshape(bsz, seq, FNET_WIDTH).astype(f.dtype)


def expert_choice_ffn(u, w_router, w_up, w_down):
    bsz, seq, d = u.shape
    n = bsz * seq
    cap = (EC_CAPACITY_FACTOR * n) // N_EXPERTS
    flat = u.reshape(n, d)
    affinity = jax.nn.softmax(jnp.einsum("nd,de->ne", flat, w_router, preferred_element_type=jnp.float32), axis=-1)
    gate, idx = lax.top_k(affinity.T, cap)
    xe = flat[idx]
    h = jnp.einsum("ecd,edf->ecf", xe, w_up)
    h_gate, h_up = jnp.split(h, 2, axis=-1)
    ye = jnp.einsum("ecf,efd->ecd", jax.nn.silu(h_gate) * h_up, w_down) * gate[..., None].astype(u.dtype)
    out = jnp.zeros((n, d), u.dtype).at[idx.reshape(-1)].add(ye.reshape(-1, d))
    return out.reshape(bsz, seq, d)


def encoder_trunk(x, norm_mix, w_in, conv_w, conv_b, ssd_dt_bias, ssd_a_log, ssd_d, ssd_norm,
                  ret_log2_decay, attn_q_norm, attn_k_norm, w_branch, w_out, norm_ffn,
                  w_router, w_up, w_down, norm_final):
    bsz, seq, _ = x.shape
    rows = seq // GRID_W
    ret_row, ret_col = axial_angles(rows, RET_QK_DIM)
    att_row, att_col = axial_angles(rows, ATT_HEAD_DIM)
    offsets = np.cumsum(IN_SIZES)[:-1].tolist()
    for l in range(DEPTH):
        u = rms_norm(x, norm_mix[l])
        proj = jnp.einsum("bld,de->ble", u, w_in[l])
        (z, xbc, dt_f, dt_b, r_q, r_k, r_v, r_g, a_q, a_k, a_v, f_in, gate_logits) = jnp.split(proj, offsets, axis=-1)
        o_ssd = ssd_mixer(z, xbc, dt_f, dt_b, conv_w[l], conv_b[l], ssd_dt_bias[l], ssd_a_log[l], ssd_d[l], ssd_norm[l])
        o_ret = retention_mixer(r_q, r_k, r_v, r_g, ret_log2_decay[l], ret_row, ret_col)
        o_att = attention_mixer(a_q, a_k, a_v, attn_q_norm[l], attn_k_norm[l], att_row, att_col)
        o_fft = fourier_mixer(f_in)
        gates = jax.nn.sigmoid(gate_logits.astype(jnp.float32)).reshape(bsz, seq, N_BRANCHES, D_MODEL)
        branches = (o_ssd, o_ret, o_att, o_fft)
        merged = gates[:, :, 0] * jnp.einsum("blc,cd->bld", branches[0], w_branch[l, 0], preferred_element_type=jnp.float32)
        for k_br in range(1, N_BRANCHES):
            merged = merged + gates[:, :, k_br] * jnp.einsum("blc,cd->bld", branches[k_br], w_branch[l, k_br], preferred_element_type=jnp.float32)
        x = x + jnp.einsum("bld,de->ble", merged.astype(x.dtype), w_out[l])
        x = x + expert_choice_ffn(rms_norm(x, norm_ffn[l]), w_router[l], w_up[l], w_down[l])
    return rms_norm(x, norm_final)


def setup_inputs(seed: int = 0) -> dict:
    key = jax.random.key(seed)
    ks = jax.random.split(key, 20)
    f32 = jnp.float32
    nrm = lambda k, shape, scale: jax.random.normal(k, shape, f32) * scale
    x_prompt = jax.random.normal(ks[0], (BATCH, SEQ, D_MODEL), f32)
    x_sample = jax.random.normal(ks[1], (DEC_BATCH, DEC_SEQ, D_MODEL), f32)
    norm_mix = 1.0 + nrm(ks[2], (DEPTH, D_MODEL), 0.02)
    w_in = nrm(ks[3], (DEPTH, D_MODEL, D_IN), D_MODEL ** -0.5)
    conv_w = nrm(ks[4], (DEPTH, SSD_CONV, SSD_XBC), SSD_CONV ** -0.5)
    conv_b = nrm(ks[5], (DEPTH, SSD_XBC), 0.02)
    dt0 = jnp.exp(jax.random.uniform(ks[6], (DEPTH, 2, SSD_HEADS), f32, math.log(1e-3), math.log(1e-1)))
    ssd_dt_bias = dt0 + jnp.log(-jnp.expm1(-dt0))
    ssd_a_log = jnp.log(jax.random.uniform(ks[7], (DEPTH, 2, SSD_HEADS), f32, 1.0, 16.0))
    ssd_d = 1.0 + nrm(ks[8], (DEPTH, SSD_HEADS), 0.1)
    ssd_norm = 1.0 + nrm(ks[9], (DEPTH, SSD_WIDTH), 0.02)
    ret_log2_decay = (-5.0 - jnp.arange(RET_HEADS, dtype=f32)) + nrm(ks[10], (DEPTH, 2, RET_HEADS), 0.1)
    attn_q_norm = 1.0 + nrm(ks[11], (DEPTH, ATT_HEAD_DIM), 0.02)
    attn_k_norm = 1.0 + nrm(ks[12], (DEPTH, ATT_HEAD_DIM), 0.02)
    w_branch = nrm(ks[13], (DEPTH, N_BRANCHES, BRANCH_WIDTH, D_MODEL), BRANCH_WIDTH ** -0.5)
    w_out = nrm(ks[14], (DEPTH, D_MODEL, D_MODEL), D_MODEL ** -0.5)
    norm_ffn = 1.0 + nrm(ks[15], (DEPTH, D_MODEL), 0.02)
    w_router = nrm(ks[16], (DEPTH, D_MODEL, N_EXPERTS), D_MODEL ** -0.5)
    w_up = nrm(ks[17], (DEPTH, N_EXPERTS, D_MODEL, 2 * EXPERT_FF), D_MODEL ** -0.5)
    w_down = nrm(ks[18], (DEPTH, N_EXPERTS, EXPERT_FF, D_MODEL), EXPERT_FF ** -0.5)
    norm_final = 1.0 + nrm(ks[19], (D_MODEL,), 0.02)
    return {"x_prompt": x_prompt, "x_sample": x_sample, "norm_mix": norm_mix, "w_in": w_in,
            "conv_w": conv_w, "conv_b": conv_b, "ssd_dt_bias": ssd_dt_bias, "ssd_a_log": ssd_a_log,
            "ssd_d": ssd_d, "ssd_norm": ssd_norm, "ret_log2_decay": ret_log2_decay,
            "attn_q_norm": attn_q_norm, "attn_k_norm": attn_k_norm, "w_branch": w_branch, "w_out": w_out,
            "norm_ffn": norm_ffn, "w_router": w_router, "w_up": w_up, "w_down": w_down,
            "norm_final": norm_final}


def reference(x_prompt, x_sample, norm_mix, w_in, conv_w, conv_b, ssd_dt_bias, ssd_a_log, ssd_d, ssd_norm,
              ret_log2_decay, attn_q_norm, attn_k_norm, w_branch, w_out, norm_ffn, w_router, w_up, w_down,
              norm_final):
    y_prompt = encoder_trunk(x_prompt, norm_mix, w_in, conv_w, conv_b, ssd_dt_bias, ssd_a_log, ssd_d, ssd_norm,
                             ret_log2_decay, attn_q_norm, attn_k_norm, w_branch, w_out, norm_ffn,
                             w_router, w_up, w_down, norm_final)
    y_sample = encoder_trunk(x_sample, norm_mix, w_in, conv_w, conv_b, ssd_dt_bias, ssd_a_log, ssd_d, ssd_norm,
                             ret_log2_decay, attn_q_norm, attn_k_norm, w_branch, w_out, norm_ffn,
                             w_router, w_up, w_down, norm_final)
    return (y_prompt, y_sample)
```

```python
import functools
import math

import numpy as np
import jax
import jax.numpy as jnp
from jax import lax
from jax.experimental import pallas as pl
from jax.experimental.pallas import tpu as pltpu

F32 = jnp.float32
BF16 = jnp.bfloat16

D_MODEL = 1024
DEPTH = 4
GRID_W = 64
CHUNK = 128
EPS = 1e-6
ROPE_THETA = 10000.0

SSD_HEADS = 8
SSD_HEAD_DIM = 64
SSD_WIDTH = 512
SSD_GROUPS = 2
SSD_STATE = 128
SSD_CONV = 5
SSD_XBC = 1024

RET_HEADS = 4
RET_DIM = 128

ATT_HEADS = 8
ATT_KV_HEADS = 2
ATT_HEAD_DIM = 64

FNET_GROUPS = 4
FNET_GROUP_DIM = 128

N_BRANCHES = 4
N_EXPERTS = 16
EXPERT_FF = 1024
EC_CAPACITY_FACTOR = 2

IN_SIZES = (512, 1024, 8, 8, 512, 512, 512, 512, 512, 128, 128, 512, 4096)
D_IN = sum(IN_SIZES)

LANES = 128
CONV_HALO = 16

COL_XBC, COL_Z, COL_RQ, COL_RK, COL_RV, COL_RG, COL_AQ = 0, 8, 12, 16, 20, 24, 28
COL_GATES, COL_F, COL_AK, COL_AV, COL_DT = 32, 64, 68, 69, 70
PROJ_COLS = 72 * LANES


def _proj_column_perm():
    offs = np.concatenate([[0], np.cumsum(IN_SIZES)])
    (z, xbc, dtf, dtb, rq, rk, rv, rg, aq, ak, av, f, gates) = [np.arange(offs[i], offs[i + 1]) for i in range(13)]
    pad = lambda n: np.full((n,), D_IN)
    small = np.concatenate([ak, av, dtf, dtb, pad(LANES - 16), pad(LANES)])
    perm = np.concatenate([xbc, z, rq, rk, rv, rg, aq, gates, f, small])
    assert perm.shape[0] == PROJ_COLS
    return perm


_PROJ_PERM = _proj_column_perm()


def _params(*sem):
    return pltpu.CompilerParams(dimension_semantics=sem, vmem_limit_bytes=56 * 1024 * 1024)


def _silu(x):
    return x * jax.nn.sigmoid(x)


def _proj_kernel(x_ref, nw_ref, w_ref, o_ref, u_scr):
    @pl.when(pl.program_id(1) == 0)
    def _():
        x = x_ref[...]
        ms = jnp.mean(x * x, axis=-1, keepdims=True)
        u_scr[...] = (x * lax.rsqrt(ms + EPS) * nw_ref[...]).astype(BF16)

    o_ref[...] = jnp.dot(u_scr[...], w_ref[...], preferred_element_type=F32).astype(o_ref.dtype)


def _norm_proj(x2, nw, w_all, layer):
    n, d = x2.shape
    cols = w_all.shape[-1]
    tm = min(1024, n)
    tn = 1536
    return pl.pallas_call(
        _proj_kernel,
        grid=(n // tm, cols // tn),
        in_specs=[pl.BlockSpec((tm, d), lambda i, j: (i, 0)),
                  pl.BlockSpec((None, 1, d), lambda i, j: (layer, 0, 0)),
                  pl.BlockSpec((None, d, tn), lambda i, j: (layer, 0, j))],
        out_specs=pl.BlockSpec((tm, tn), lambda i, j: (i, j)),
        out_shape=jax.ShapeDtypeStruct((n, cols), BF16),
        scratch_shapes=[pltpu.VMEM((tm, d), BF16)],
        compiler_params=_params("arbitrary", "arbitrary"),
        name="norm_proj",
    )(x2, nw, w_all)


def _cumsum_rows(x, reverse):
    n = x.shape[0]
    row = lax.broadcasted_iota(jnp.int32, x.shape, 0)
    d = 1
    while d < n:
        if reverse:
            x = x + jnp.where(row < n - d, pltpu.roll(x, n - d, axis=0), 0.0)
        else:
            x = x + jnp.where(row >= d, pltpu.roll(x, d, axis=0), 0.0)
        d *= 2
    return x


def _softplus(x):
    return jnp.maximum(x, 0.0) + jnp.log1p(jnp.exp(-jnp.abs(x)))


def _ssd_kernel(rev, *refs):
    if rev:
        (prev_ref, main_ref, next_ref, dt_ref, cw_ref, cb_ref, dtb_ref, alog_ref, dsk_ref,
         z_ref, yf_ref, nw_ref, o_ref, ext_scr, h_scr) = refs
    else:
        (prev_ref, main_ref, next_ref, dt_ref, cw_ref, cb_ref, dtb_ref, alog_ref, dsk_ref,
         o_ref, ext_scr, h_scr) = refs
    c = pl.program_id(1)
    nc = pl.num_programs(1)
    cc = nc - 1 - c if rev else c

    @pl.when(c == 0)
    def _():
        h_scr[...] = jnp.zeros_like(h_scr)

    ext_scr[0:CONV_HALO, :] = jnp.where(cc == 0, 0.0, prev_ref[...].astype(F32))
    ext_scr[CONV_HALO:CONV_HALO + CHUNK, :] = main_ref[...].astype(F32)
    ext_scr[CONV_HALO + CHUNK:, :] = jnp.where(cc == nc - 1, 0.0, next_ref[...].astype(F32))
    acc = jnp.broadcast_to(cb_ref[...], (CHUNK, SSD_XBC))
    for k in range(SSD_CONV):
        acc = acc + cw_ref[k:k + 1, :] * ext_scr[pl.ds(CONV_HALO - SSD_CONV // 2 + k, CHUNK), :]
    act = _silu(acc)
    gn = SSD_GROUPS * SSD_STATE
    x = act[:, :SSD_WIDTH]
    b_mat = act[:, SSD_WIDTH:SSD_WIDTH + gn]
    c_mat = act[:, SSD_WIDTH + gn:]

    off = SSD_HEADS if rev else 0
    dt = _softplus(dt_ref[...].astype(F32) + dtb_ref[...])
    a = -jnp.exp(alog_ref[...])
    cs = _cumsum_rows(dt * a, rev)
    cs_t = cs.T
    rowi = lax.broadcasted_iota(jnp.int32, (CHUNK, CHUNK), 0)
    coli = lax.broadcasted_iota(jnp.int32, (CHUNK, CHUNK), 1)
    mask = (rowi < coli) if rev else (rowi >= coli)

    def expand(v):
        return jnp.concatenate(
            [jnp.broadcast_to(v[:, off + h:off + h + 1], (CHUNK, SSD_HEAD_DIM)) for h in range(SSD_HEADS)], axis=1)

    dt_e = expand(dt)
    cs_e = expand(cs)
    edge = cs_e[0:1, :] if rev else cs_e[CHUNK - 1:CHUNK, :]
    xdt = x * dt_e
    xst = (xdt * jnp.exp(edge - cs_e)).astype(BF16)
    xdt = xdt.astype(BF16)
    e_in = jnp.exp(cs_e)
    e_chunk = jnp.exp(edge)

    r = SSD_HEADS // SSD_GROUPS
    gw = r * SSD_HEAD_DIM
    ys = []
    for g in range(SSD_GROUPS):
        bg = b_mat[:, g * SSD_STATE:(g + 1) * SSD_STATE].astype(BF16)
        cg = c_mat[:, g * SSD_STATE:(g + 1) * SSD_STATE].astype(BF16)
        scores = lax.dot_general(cg, bg, (((1,), (1,)), ((), ())), preferred_element_type=F32)
        h_prev = h_scr[g]
        y_off = jnp.dot(cg, h_prev.astype(BF16), preferred_element_type=F32) * e_in[:, g * gw:(g + 1) * gw]
        st = lax.dot_general(bg, xst[:, g * gw:(g + 1) * gw], (((0,), (0,)), ((), ())),
                             preferred_element_type=F32)
        h_scr[g] = h_prev * e_chunk[:, g * gw:(g + 1) * gw] + st
        for j in range(r):
            h = g * r + j
            colb = jnp.broadcast_to(cs[:, off + h:off + h + 1], (CHUNK, CHUNK))
            rowb = cs_t[off + h:off + h + 1, :]
            decay = jnp.exp(jnp.where(mask, colb - rowb, -jnp.inf))
            m = (scores * decay).astype(BF16)
            y_d = jnp.dot(m, xdt[:, h * SSD_HEAD_DIM:(h + 1) * SSD_HEAD_DIM], preferred_element_type=F32)
            ys.append(y_d + y_off[:, j * SSD_HEAD_DIM:(j + 1) * SSD_HEAD_DIM])
    y = jnp.concatenate(ys, axis=1)

    if rev:
        y = (y + yf_ref[...]) * _silu(z_ref[...].astype(F32))
        ms = jnp.mean(y * y, axis=-1, keepdims=True)
        o_ref[...] = (y * lax.rsqrt(ms + EPS) * nw_ref[...]).astype(o_ref.dtype)
    else:
        o_ref[...] = y + x * dsk_ref[...]


def _ssd_mixer(proj, bsz, seq, conv_w, conv_b, dtb_row, alog_row, dsk_row, norm_w):
    n = bsz * seq
    nc = seq // CHUNK
    hb = CHUNK // CONV_HALO
    nhalo = n // CONV_HALO

    def call(rev, extra_in, extra_cols, out_dtype):
        chunk = (lambda c: nc - 1 - c) if rev else (lambda c: c)
        row = lambda b, c: b * nc + chunk(c)
        in_specs = [
            pl.BlockSpec((CONV_HALO, SSD_XBC), lambda b, c: (jnp.maximum(row(b, c) * hb - 1, 0), COL_XBC // 8)),
            pl.BlockSpec((CHUNK, SSD_XBC), lambda b, c: (row(b, c), COL_XBC // 8)),
            pl.BlockSpec((CONV_HALO, SSD_XBC), lambda b, c: (jnp.minimum((row(b, c) + 1) * hb, nhalo - 1), COL_XBC // 8)),
            pl.BlockSpec((CHUNK, LANES), lambda b, c: (row(b, c), COL_DT)),
            pl.BlockSpec((SSD_CONV, SSD_XBC), lambda b, c: (0, 0)),
            pl.BlockSpec((1, SSD_XBC), lambda b, c: (0, 0)),
            pl.BlockSpec((1, LANES), lambda b, c: (0, 0)),
            pl.BlockSpec((1, LANES), lambda b, c: (0, 0)),
            pl.BlockSpec((1, SSD_WIDTH), lambda b, c: (0, 0)),
        ] + [pl.BlockSpec((CHUNK, SSD_WIDTH), (lambda cb: (lambda b, c: (row(b, c), cb)))(cb)) for cb in extra_cols]
        if rev:
            in_specs.append(pl.BlockSpec((1, SSD_WIDTH), lambda b, c: (0, 0)))
        return pl.pallas_call(
            functools.partial(_ssd_kernel, rev),
            grid=(bsz, nc),
            in_specs=in_specs,
            out_specs=pl.BlockSpec((CHUNK, SSD_WIDTH), lambda b, c: (row(b, c), 0)),
            out_shape=jax.ShapeDtypeStruct((n, SSD_WIDTH), out_dtype),
            scratch_shapes=[pltpu.VMEM((CHUNK + 2 * CONV_HALO, SSD_XBC), F32),
                            pltpu.VMEM((SSD_GROUPS, SSD_STATE, SSD_WIDTH // SSD_GROUPS), F32)],
            compiler_params=_params("arbitrary", "arbitrary"),
            name="ssd_bwd" if rev else "ssd_fwd",
        )(proj, proj, proj, proj, conv_w, conv_b, dtb_row, alog_row, dsk_row, *extra_in)

    y_f = call(False, (), (), F32)
    return call(True, (proj, y_f, norm_w), (COL_Z // 4, 0), BF16)


def _rotary_tables(seq, head_dim):
    n_freq = head_dim // 4
    freqs = ROPE_THETA ** (-jnp.arange(n_freq, dtype=F32) / n_freq)
    t = jnp.arange(seq)
    ang_row = (t // GRID_W).astype(F32)[:, None] * freqs
    ang_col = (t % GRID_W).astype(F32)[:, None] * freqs
    cos = jnp.concatenate([jnp.cos(ang_row)] * 2 + [jnp.cos(ang_col)] * 2, axis=1)
    sin = jnp.concatenate([-jnp.sin(ang_row), jnp.sin(ang_row), -jnp.sin(ang_col), jnp.sin(ang_col)], axis=1)
    return cos, sin


def _rotate(x, cos, sin_signed, quarter):
    w = x.shape[1]
    lane = lax.broadcasted_iota(jnp.int32, x.shape, 1)
    partner = jnp.where((lane % (2 * quarter)) < quarter,
                        pltpu.roll(x, w - quarter, axis=1), pltpu.roll(x, quarter, axis=1))
    return x * cos + partner * sin_signed


def _ret_kernel(rev, *refs):
    if rev:
        (lg_ref, q_ref, k_ref, v_ref, cos_ref, sin_ref, g_ref, yf_ref, o_ref, tab_scr, h_scr) = refs
    else:
        (lg_ref, q_ref, k_ref, v_ref, cos_ref, sin_ref, o_ref, tab_scr, h_scr) = refs
    first = jnp.logical_and(pl.program_id(0) == 0, pl.program_id(1) == 0)

    @pl.when(first)
    def _():
        rowi = lax.broadcasted_iota(jnp.int32, (CHUNK, CHUNK), 0)
        coli = lax.broadcasted_iota(jnp.int32, (CHUNK, CHUNK), 1)
        rowf = rowi.astype(F32)
        for h in range(RET_HEADS):
            lg = lg_ref[(RET_HEADS if rev else 0) + h]
            if rev:
                dist = jnp.where(coli > rowi, (coli - rowi).astype(F32) * lg, -jnp.inf)
                kd = rowf * lg
                qd = (CHUNK - rowf) * lg
            else:
                dist = jnp.where(rowi >= coli, (rowi - coli).astype(F32) * lg, -jnp.inf)
                kd = (CHUNK - 1 - rowf) * lg
                qd = (rowf + 1.0) * lg
            tab_scr[0, h] = jnp.exp(dist)
            tab_scr[1, h] = jnp.exp(kd)
            tab_scr[2, h] = jnp.exp(qd)
            tab_scr[3, h] = jnp.exp(jnp.full((CHUNK, CHUNK), CHUNK, F32) * lg)

    @pl.when(pl.program_id(1) == 0)
    def _():
        h_scr[...] = jnp.zeros_like(h_scr)

    cos = jnp.concatenate([cos_ref[...]] * RET_HEADS, axis=1)
    sin = jnp.concatenate([sin_ref[...]] * RET_HEADS, axis=1)
    q = _rotate(q_ref[...].astype(F32), cos, sin, RET_DIM // 4)
    k = _rotate(k_ref[...].astype(F32), cos, sin, RET_DIM // 4) * (RET_DIM ** -0.5)
    v = v_ref[...].astype(F32)
    ys = []
    for h in range(RET_HEADS):
        sl = slice(h * RET_DIM, (h + 1) * RET_DIM)
        qh = q[:, sl].astype(BF16)
        kh = k[:, sl].astype(BF16)
        vh = v[:, sl]
        s = lax.dot_general(qh, kh, (((1,), (1,)), ((), ())), preferred_element_type=F32) * tab_scr[0, h]
        inner = jnp.dot(s.astype(BF16), vh.astype(BF16), preferred_element_type=F32)
        h_prev = h_scr[h]
        cross = jnp.dot(qh, h_prev.astype(BF16), preferred_element_type=F32) * tab_scr[2, h]
        st = lax.dot_general(kh, (vh * tab_scr[1, h]).astype(BF16), (((0,), (0,)), ((), ())),
                             preferred_element_type=F32)
        h_scr[h] = h_prev * tab_scr[3, h] + st
        ys.append(inner + cross)

    if rev:
        outs = []
        for h in range(RET_HEADS):
            sl = slice(h * RET_DIM, (h + 1) * RET_DIM)
            y = ys[h] + yf_ref[:, sl]
            y = y * lax.rsqrt(jnp.mean(y * y, axis=-1, keepdims=True) + EPS)
            outs.append(y * _silu(g_ref[:, sl].astype(F32)))
        o_ref[...] = jnp.concatenate(outs, axis=1).astype(o_ref.dtype)
    else:
        o_ref[...] = jnp.concatenate(ys, axis=1)


def _ret_mixer(proj, bsz, seq, log_gamma, cos, sin):
    n = bsz * seq
    nc = seq // CHUNK
    width = RET_HEADS * RET_DIM

    def call(rev, extra_in, extra_cols, out_dtype):
        chunk = (lambda c: nc - 1 - c) if rev else (lambda c: c)
        row = lambda b, c: b * nc + chunk(c)
        pcol = lambda cb: pl.BlockSpec((CHUNK, width), lambda b, c: (row(b, c), cb))
        tab = pl.BlockSpec((CHUNK, RET_DIM), lambda b, c: (chunk(c), 0))
        in_specs = [pl.BlockSpec(memory_space=pltpu.SMEM),
                    pcol(COL_RQ // 4), pcol(COL_RK // 4), pcol(COL_RV // 4), tab, tab] + [pcol(cb) for cb in extra_cols]
        return pl.pallas_call(
            functools.partial(_ret_kernel, rev),
            grid=(bsz, nc),
            in_specs=in_specs,
            out_specs=pl.BlockSpec((CHUNK, width), lambda b, c: (row(b, c), 0)),
            out_shape=jax.ShapeDtypeStruct((n, width), out_dtype),
            scratch_shapes=[pltpu.VMEM((4, RET_HEADS, CHUNK, CHUNK), F32),
                            pltpu.VMEM((RET_HEADS, RET_DIM, RET_DIM), F32)],
            compiler_params=_params("arbitrary", "arbitrary"),
            name="ret_bwd" if rev else "ret_fwd",
        )(log_gamma, proj, proj, proj, cos, sin, *extra_in)

    y_f = call(False, (), (), F32)
    return call(True, (proj, y_f), (COL_RG // 4, 0), BF16)


def _head_mean_square(x, ones_bd):
    x2 = x * x
    hi = x2.astype(BF16)
    lo = (x2 - hi.astype(F32)).astype(BF16)
    tot = jnp.dot(hi, ones_bd, preferred_element_type=F32) + jnp.dot(lo, ones_bd, preferred_element_type=F32)
    return tot * (1.0 / ATT_HEAD_DIM)


def _att_prep_kernel(q_ref, k_ref, qn_ref, kn_ref, cos_ref, sin_ref, ones_ref, qo_ref, ko_ref):
    rep = ATT_HEADS // ATT_KV_HEADS
    cos1 = cos_ref[...]
    sin1 = sin_ref[...]
    q = q_ref[...].astype(F32)
    q = q * lax.rsqrt(_head_mean_square(q, ones_ref[...]) + EPS) * qn_ref[...]
    q = _rotate(q, jnp.concatenate([cos1] * rep, axis=1), jnp.concatenate([sin1] * rep, axis=1), ATT_HEAD_DIM // 4)
    qo_ref[...] = (q * (ATT_HEAD_DIM ** -0.5)).astype(qo_ref.dtype)
    k = k_ref[...].astype(F32)
    kw = ATT_KV_HEADS * ATT_HEAD_DIM
    k = k * lax.rsqrt(_head_mean_square(k, ones_ref[0:kw, 0:kw]) + EPS) * kn_ref[...]
    k = _rotate(k, cos1, sin1, ATT_HEAD_DIM // 4)
    ko_ref[...] = k.astype(ko_ref.dtype)


def _flash_kernel(q_ref, k_ref, v_ref, o_ref, m_scr, l_scr, acc_scr):
    ki = pl.program_id(2)

    @pl.when(ki == 0)
    def _():
        m_scr[...] = jnp.full_like(m_scr, -jnp.inf)
        l_scr[...] = jnp.zeros_like(l_scr)
        acc_scr[...] = jnp.zeros_like(acc_scr)

    rep = ATT_HEADS // ATT_KV_HEADS
    for h in range(ATT_HEADS):
        g = h // rep
        qh = q_ref[:, h * ATT_HEAD_DIM:(h + 1) * ATT_HEAD_DIM]
        kg = k_ref[:, g * ATT_HEAD_DIM:(g + 1) * ATT_HEAD_DIM]
        vg = v_ref[:, g * ATT_HEAD_DIM:(g + 1) * ATT_HEAD_DIM]
        s = lax.dot_general(qh, kg, (((1,), (1,)), ((), ())), preferred_element_type=F32)
        m_prev = m_scr[h]
        m_new = jnp.maximum(m_prev, jnp.max(s, axis=1, keepdims=True))
        p = jnp.exp(s - m_new)
        alpha = jnp.exp(m_prev - m_new)
        l_scr[h] = alpha * l_scr[h] + jnp.sum(p, axis=1, keepdims=True)
        acc_scr[h] = alpha * acc_scr[h] + jnp.dot(p.astype(BF16), vg, preferred_element_type=F32)
        m_scr[h] = m_new

    @pl.when(ki == pl.num_programs(2) - 1)
    def _():
        o_ref[...] = jnp.concatenate([acc_scr[h] / l_scr[h] for h in range(ATT_HEADS)], axis=1).astype(o_ref.dtype)


def _att_mixer(proj, bsz, seq, q_norm, k_norm, cos, sin, ones_bd):
    n = bsz * seq
    qw = ATT_HEADS * ATT_HEAD_DIM
    kw = ATT_KV_HEADS * ATT_HEAD_DIM
    tm = min(512, seq)
    nt = seq // tm
    qn_row = jnp.tile(q_norm, ATT_HEADS)[None, :]
    kn_row = jnp.tile(k_norm, ATT_KV_HEADS)[None, :]
    q_rot, k_rot = pl.pallas_call(
        _att_prep_kernel,
        grid=(bsz, nt),
        in_specs=[pl.BlockSpec((tm, qw), lambda b, i: (b * nt + i, COL_AQ // 4)),
                  pl.BlockSpec((tm, kw), lambda b, i: (b * nt + i, COL_AK)),
                  pl.BlockSpec((1, qw), lambda b, i: (0, 0)),
                  pl.BlockSpec((1, kw), lambda b, i: (0, 0)),
                  pl.BlockSpec((tm, kw), lambda b, i: (i, 0)),
                  pl.BlockSpec((tm, kw), lambda b, i: (i, 0)),
                  pl.BlockSpec((qw, qw), lambda b, i: (0, 0))],
        out_specs=[pl.BlockSpec((tm, qw), lambda b, i: (b * nt + i, 0)),
                   pl.BlockSpec((tm, kw), lambda b, i: (b * nt + i, 0))],
        out_shape=[jax.ShapeDtypeStruct((n, qw), BF16), jax.ShapeDtypeStruct((n, kw), BF16)],
        compiler_params=_params("arbitrary", "arbitrary"),
        name="att_prep",
    )(proj, proj, qn_row, kn_row, cos, sin, ones_bd)

    tq = min(512, seq)
    tk = min(512, seq)
    nq = seq // tq
    nk = seq // tk
    return pl.pallas_call(
        _flash_kernel,
        grid=(bsz, nq, nk),
        in_specs=[pl.BlockSpec((tq, qw), lambda b, i, j: (b * nq + i, 0)),
                  pl.BlockSpec((tk, kw), lambda b, i, j: (b * nk + j, 0)),
                  pl.BlockSpec((tk, kw), lambda b, i, j: (b * nk + j, COL_AV))],
        out_specs=pl.BlockSpec((tq, qw), lambda b, i, j: (b * nq + i, 0)),
        out_shape=jax.ShapeDtypeStruct((n, qw), BF16),
        scratch_shapes=[pltpu.VMEM((ATT_HEADS, tq, 1), F32), pltpu.VMEM((ATT_HEADS, tq, 1), F32),
                        pltpu.VMEM((ATT_HEADS, tq, ATT_HEAD_DIM), F32)],
        compiler_params=_params("arbitrary", "arbitrary", "arbitrary"),
        name="flash_attention",
    )(q_rot, k_rot, proj)


def _fft_split(seq):
    l1 = 1 << (int(math.log2(seq)) // 2)
    return l1, seq // l1


def _fft_tables(seq):
    l1, l2 = _fft_split(seq)
    c = np.arange(FNET_GROUP_DIM)
    ang = 2.0 * np.pi * ((c[:, None] * c[None, :]) % FNET_GROUP_DIM) / FNET_GROUP_DIM
    chan = np.concatenate([np.cos(ang), -np.sin(ang)], axis=1)
    k1 = np.arange(l1)
    ang1 = 2.0 * np.pi * ((k1[:, None] * k1[None, :]) % l1) / l1
    t2 = np.arange(l2)
    k2 = np.arange(l2)
    kk = k1[:, None, None] + l1 * k2[None, :, None]
    ang2 = 2.0 * np.pi * ((kk * t2[None, None, :]) % seq) / seq
    stage2 = np.concatenate([np.cos(ang2), np.sin(ang2)], axis=2)
    to = lambda a: jnp.asarray(a, dtype=BF16)
    return to(chan), to(np.cos(ang1)), to(np.sin(ang1)), to(stage2)


def _fft_a_kernel(f_ref, chan_ref, c1_ref, s1_ref, ur_ref, ui_ref):
    zr, zi = [], []
    for g in range(FNET_GROUPS):
        z = jnp.dot(f_ref[:, g * FNET_GROUP_DIM:(g + 1) * FNET_GROUP_DIM], chan_ref[...],
                    preferred_element_type=F32)
        zr.append(z[:, :FNET_GROUP_DIM])
        zi.append(z[:, FNET_GROUP_DIM:])
    zr = jnp.concatenate(zr, axis=1).astype(BF16)
    zi = jnp.concatenate(zi, axis=1).astype(BF16)
    c1 = c1_ref[...]
    s1 = s1_ref[...]
    dot = functools.partial(jnp.dot, preferred_element_type=F32)
    ur_ref[...] = (dot(c1, zr) + dot(s1, zi)).astype(ur_ref.dtype)
    ui_ref[...] = (dot(c1, zi) - dot(s1, zr)).astype(ui_ref.dtype)


def _fft_b_kernel(nk, scale, ur_ref, ui_ref, m_ref, o_ref):
    w = FNET_GROUPS * FNET_GROUP_DIM
    for j in range(nk):
        u = jnp.concatenate([ur_ref[j], ui_ref[j]], axis=0)
        y = jnp.dot(m_ref[j], u, preferred_element_type=F32)
        o_ref[:, j * w:(j + 1) * w] = (y * scale).astype(o_ref.dtype)


def _fft_mixer(proj, bsz, seq, tables):
    chan, c1, s1, stage2 = tables
    l1, l2 = _fft_split(seq)
    w = FNET_GROUPS * FNET_GROUP_DIM
    pblocks = PROJ_COLS // w
    proj3 = proj.reshape(bsz, l1, l2 * PROJ_COLS)
    u_r, u_i = pl.pallas_call(
        _fft_a_kernel,
        grid=(bsz, l2),
        in_specs=[pl.BlockSpec((None, l1, w), lambda b, t: (b, 0, t * pblocks + COL_F // 4)),
                  pl.BlockSpec((FNET_GROUP_DIM, 2 * FNET_GROUP_DIM), lambda b, t: (0, 0)),
                  pl.BlockSpec((l1, l1), lambda b, t: (0, 0)),
                  pl.BlockSpec((l1, l1), lambda b, t: (0, 0))],
        out_specs=[pl.BlockSpec((None, l1, w), lambda b, t: (b, 0, t)),
                   pl.BlockSpec((None, l1, w), lambda b, t: (b, 0, t))],
        out_shape=[jax.ShapeDtypeStruct((bsz, l1, l2 * w), BF16)] * 2,
        compiler_params=_params("arbitrary", "arbitrary"),
        name="fft_stage_a",
    )(proj3, chan, c1, s1)
    nk = min(8, l1)
    scale = 1.0 / math.sqrt(seq * FNET_GROUP_DIM)
    u_r = u_r.reshape(bsz, l1, l2, w)
    u_i = u_i.reshape(bsz, l1, l2, w)
    y = pl.pallas_call(
        functools.partial(_fft_b_kernel, nk, scale),
        grid=(bsz, l1 // nk),
        in_specs=[pl.BlockSpec((None, nk, l2, w), lambda b, i: (b, i, 0, 0)),
                  pl.BlockSpec((None, nk, l2, w), lambda b, i: (b, i, 0, 0)),
                  pl.BlockSpec((nk, l2, 2 * l2), lambda b, i: (i, 0, 0))],
        out_specs=pl.BlockSpec((None, l2, nk * w), lambda b, i: (b, 0, i)),
        out_shape=jax.ShapeDtypeStruct((bsz, l2, l1 * w), BF16),
        compiler_params=_params("arbitrary", "arbitrary"),
        name="fft_stage_b",
    )(u_r, u_i, stage2)
    return y.reshape(bsz * seq, w)


def _merge_kernel(x_ref, g0_ref, g1_ref, g2_ref, g3_ref, o0_ref, o1_ref, o2_ref, o3_ref,
                  wb_ref, wo_ref, nf_ref, wr_ref, xo_ref, u_ref, aff_ref):
    merged = None
    for kbr, (g_ref, o_ref) in enumerate(((g0_ref, o0_ref), (g1_ref, o1_ref), (g2_ref, o2_ref), (g3_ref, o3_ref))):
        term = jax.nn.sigmoid(g_ref[...].astype(F32)) * jnp.dot(o_ref[...], wb_ref[kbr], preferred_element_type=F32)
        merged = term if merged is None else merged + term
    x = x_ref[...] + jnp.dot(merged.astype(BF16), wo_ref[...], preferred_element_type=F32)
    xo_ref[...] = x
    ms = jnp.mean(x * x, axis=-1, keepdims=True)
    u = (x * lax.rsqrt(ms + EPS) * nf_ref[...]).astype(BF16)
    u_ref[...] = u
    logits = jnp.dot(u, wr_ref[...], preferred_element_type=F32)
    lane = lax.broadcasted_iota(jnp.int32, logits.shape, 1)
    logits = jnp.where(lane < N_EXPERTS, logits, -jnp.inf)
    e = jnp.exp(logits - jnp.max(logits, axis=-1, keepdims=True))
    aff_ref[...] = e / jnp.sum(e, axis=-1, keepdims=True)


def _merge(x2, proj, branches, w_branch, w_out, norm_ffn, w_router, layer):
    n, d = x2.shape
    tm = min(256, n)
    bw = SSD_WIDTH
    gate_spec = lambda kbr: pl.BlockSpec((tm, d), lambda i: (i, COL_GATES // 8 + kbr))
    br_spec = pl.BlockSpec((tm, bw), lambda i: (i, 0))
    return pl.pallas_call(
        _merge_kernel,
        grid=(n // tm,),
        in_specs=[pl.BlockSpec((tm, d), lambda i: (i, 0))] + [gate_spec(k) for k in range(N_BRANCHES)]
                 + [br_spec] * N_BRANCHES
                 + [pl.BlockSpec((None, N_BRANCHES, bw, d), lambda i: (layer, 0, 0, 0)),
                    pl.BlockSpec((None, d, d), lambda i: (layer, 0, 0)),
                    pl.BlockSpec((None, 1, d), lambda i: (layer, 0, 0)),
                    pl.BlockSpec((None, d, LANES), lambda i: (layer, 0, 0))],
        out_specs=[pl.BlockSpec((tm, d), lambda i: (i, 0)), pl.BlockSpec((tm, d), lambda i: (i, 0)),
                   pl.BlockSpec((tm, LANES), lambda i: (i, 0))],
        out_shape=[jax.ShapeDtypeStruct((n, d), F32), jax.ShapeDtypeStruct((n, d), BF16),
                   jax.ShapeDtypeStruct((n, LANES), F32)],
        compiler_params=_params("arbitrary"),
        name="merge_out",
    )(x2, proj, proj, proj, proj, *branches, w_branch, w_out, norm_ffn, w_router)


def _ffn_kernel(xe_ref, gate_ref, wu_ref, wd_ref, o_ref):
    h = jnp.dot(xe_ref[...], wu_ref[...], preferred_element_type=F32)
    act = (_silu(h[:, :EXPERT_FF]) * h[:, EXPERT_FF:]).astype(BF16)
    o_ref[...] = jnp.dot(act, wd_ref[...], preferred_element_type=F32) * gate_ref[...]


def _expert_ffn(xe, gate, w_up, w_down, layer):
    e, cap, d = xe.shape
    tc = min(512, cap)
    return pl.pallas_call(
        _ffn_kernel,
        grid=(e, cap // tc),
        in_specs=[pl.BlockSpec((None, tc, d), lambda i, j: (i, j, 0)),
                  pl.BlockSpec((None, tc, 1), lambda i, j: (i, j, 0)),
                  pl.BlockSpec((None, None, d, 2 * EXPERT_FF), lambda i, j: (layer, i, 0, 0)),
                  pl.BlockSpec((None, None, EXPERT_FF, d), lambda i, j: (layer, i, 0, 0))],
        out_specs=pl.BlockSpec((None, tc, d), lambda i, j: (i, j, 0)),
        out_shape=jax.ShapeDtypeStruct((e, cap, d), F32),
        compiler_params=_params("arbitrary", "arbitrary"),
        name="expert_ffn",
    )(xe, gate, w_up, w_down)


def _norm_kernel(x_ref, w_ref, o_ref):
    x = x_ref[...]
    ms = jnp.mean(x * x, axis=-1, keepdims=True)
    o_ref[...] = x * lax.rsqrt(ms + EPS) * w_ref[...]


def _final_norm(x2, w):
    n, d = x2.shape
    tm = min(1024, n)
    return pl.pallas_call(
        _norm_kernel,
        grid=(n // tm,),
        in_specs=[pl.BlockSpec((tm, d), lambda i: (i, 0)), pl.BlockSpec((1, d), lambda i: (0, 0))],
        out_specs=pl.BlockSpec((tm, d), lambda i: (i, 0)),
        out_shape=jax.ShapeDtypeStruct((n, d), F32),
        compiler_params=_params("arbitrary"),
        name="final_norm",
    )(x2, w)


def _pad_lanes(v):
    return jnp.pad(v, (0, LANES - v.shape[0]))[None, :]


def _prepare(norm_mix, w_in, conv_w, conv_b, ssd_dt_bias, ssd_a_log, ssd_d, ssd_norm, ret_log2_decay,
             attn_q_norm, attn_k_norm, w_branch, w_out, norm_ffn, w_router, w_up, w_down, norm_final):
    w_pad = jnp.concatenate([w_in, jnp.zeros((DEPTH, D_MODEL, 1), w_in.dtype)], axis=-1)
    p = dict(
        norm_mix=norm_mix[:, None, :],
        w_proj=jnp.take(w_pad, jnp.asarray(_PROJ_PERM), axis=-1).astype(BF16),
        conv_w=conv_w, conv_b=conv_b[:, None, :],
        dtb=[_pad_lanes(ssd_dt_bias[l].reshape(-1)) for l in range(DEPTH)],
        alog=[_pad_lanes(ssd_a_log[l].reshape(-1)) for l in range(DEPTH)],
        dsk=[jnp.repeat(ssd_d[l], SSD_HEAD_DIM)[None, :] for l in range(DEPTH)],
        ssd_norm=ssd_norm[:, None, :],
        log_gamma=jnp.log1p(-jnp.exp2(ret_log2_decay.astype(F32))).reshape(DEPTH, 2 * RET_HEADS),
        q_norm=attn_q_norm, k_norm=attn_k_norm,
        w_branch=w_branch.astype(BF16), w_out=w_out.astype(BF16),
        norm_ffn=norm_ffn[:, None, :],
        w_router=jnp.pad(w_router, ((0, 0), (0, 0), (0, LANES - N_EXPERTS))).astype(BF16),
        w_up=w_up.astype(BF16), w_down=w_down.astype(BF16),
        norm_final=norm_final[None, :],
    )
    qw = ATT_HEADS * ATT_HEAD_DIM
    hid = np.arange(qw) // ATT_HEAD_DIM
    p["ones_bd"] = jnp.asarray(hid[:, None] == hid[None, :], dtype=BF16)
    return p


def _trunk(x, p):
    bsz, seq, d = x.shape
    n = bsz * seq
    cap = (EC_CAPACITY_FACTOR * n) // N_EXPERTS
    ret_cos, ret_sin = _rotary_tables(seq, RET_DIM)
    att_cos, att_sin = [jnp.tile(t, (1, ATT_KV_HEADS)) for t in _rotary_tables(seq, ATT_HEAD_DIM)]
    fft_tables = _fft_tables(seq)
    x2 = x.reshape(n, d)
    for l in range(DEPTH):
        proj = _norm_proj(x2, p["norm_mix"], p["w_proj"], l)
        o_ssd = _ssd_mixer(proj, bsz, seq, p["conv_w"][l], p["conv_b"][l], p["dtb"][l], p["alog"][l], p["dsk"][l],
                           p["ssd_norm"][l])
        o_ret = _ret_mixer(proj, bsz, seq, p["log_gamma"][l], ret_cos, ret_sin)
        o_att = _att_mixer(proj, bsz, seq, p["q_norm"][l], p["k_norm"][l], att_cos, att_sin, p["ones_bd"])
        o_fft = _fft_mixer(proj, bsz, seq, fft_tables)
        x2, u, aff = _merge(x2, proj, (o_ssd, o_ret, o_att, o_fft), p["w_branch"], p["w_out"], p["norm_ffn"],
                            p["w_router"], l)
        gate, idx = lax.top_k(aff[:, :N_EXPERTS].T, cap)
        xe = jnp.take(u, idx, axis=0)
        ye = _expert_ffn(xe, gate[..., None], p["w_up"], p["w_down"], l)
        x2 = x2.at[idx.reshape(-1)].add(ye.reshape(-1, d))
    return _final_norm(x2, p["norm_final"]).reshape(bsz, seq, d)


def kernel(x_prompt, x_sample, norm_mix, w_in, conv_w, conv_b, ssd_dt_bias, ssd_a_log, ssd_d, ssd_norm, ret_log2_decay, attn_q_norm, attn_k_norm, w_branch, w_out, norm_ffn, w_router, w_up, w_down, norm_final):
    p = _prepare(norm_mix, w_in, conv_w, conv_b, ssd_dt_bias, ssd_a_log, ssd_d, ssd_norm, ret_log2_decay,
                 attn_q_norm, attn_k_norm, w_branch, w_out, norm_ffn, w_router, w_up, w_down, norm_final)
    return (_trunk(x_prompt, p), _trunk(x_sample, p))
```

```python
import functools
import math

import numpy as np
import jax
import jax.numpy as jnp
from jax import lax
from jax.experimental import pallas as pl
from jax.experimental.pallas import tpu as pltpu

F32 = jnp.float32
BF16 = jnp.bfloat16

D_MODEL = 1024
DEPTH = 4
GRID_W = 64
CHUNK = 128
EPS = 1e-6
ROPE_THETA = 10000.0

SSD_HEADS = 8
SSD_HEAD_DIM = 64
SSD_WIDTH = 512
SSD_GROUPS = 2
SSD_STATE = 128
SSD_CONV = 5
SSD_XBC = 1024

RET_HEADS = 4
RET_DIM = 128

ATT_HEADS = 8
ATT_KV_HEADS = 2
ATT_HEAD_DIM = 64

FNET_GROUPS = 4
FNET_GROUP_DIM = 128

N_BRANCHES = 4
N_EXPERTS = 16
EXPERT_FF = 1024
EC_CAPACITY_FACTOR = 2

IN_SIZES = (512, 1024, 8, 8, 512, 512, 512, 512, 512, 128, 128, 512, 4096)
D_IN = sum(IN_SIZES)

LANES = 128
CONV_HALO = 16

COL_XBC, COL_Z, COL_RQ, COL_RK, COL_RV, COL_RG, COL_AQ = 0, 8, 12, 16, 20, 24, 28
COL_GATES, COL_F, COL_AK, COL_AV, COL_DT = 32, 64, 68, 69, 70
PROJ_COLS = 72 * LANES


def _proj_column_perm():
    offs = np.concatenate([[0], np.cumsum(IN_SIZES)])
    (z, xbc, dtf, dtb, rq, rk, rv, rg, aq, ak, av, f, gates) = [np.arange(offs[i], offs[i + 1]) for i in range(13)]
    pad = lambda n: np.full((n,), D_IN)
    small = np.concatenate([ak, av, dtf, dtb, pad(LANES - 16), pad(LANES)])
    perm = np.concatenate([xbc, z, rq, rk, rv, rg, aq, gates, f, small])
    assert perm.shape[0] == PROJ_COLS
    return perm


_PROJ_PERM = _proj_column_perm()


def _params(*sem):
    return pltpu.CompilerParams(dimension_semantics=sem, vmem_limit_bytes=56 * 1024 * 1024)


def _silu(x):
    return x * jax.nn.sigmoid(x)


def _proj_kernel(x_ref, nw_ref, w_ref, o_ref, u_scr):
    @pl.when(pl.program_id(1) == 0)
    def _():
        x = x_ref[...]
        ms = jnp.mean(x * x, axis=-1, keepdims=True)
        u_scr[...] = (x * lax.rsqrt(ms + EPS) * nw_ref[...]).astype(BF16)

    o_ref[...] = jnp.dot(u_scr[...], w_ref[...], preferred_element_type=F32).astype(o_ref.dtype)


def _norm_proj(x2, nw, w_all, layer):
    n, d = x2.shape
    cols = w_all.shape[-1]
    tm = min(1024, n)
    tn = 1536
    return pl.pallas_call(
        _proj_kernel,
        grid=(n // tm, cols // tn),
        in_specs=[pl.BlockSpec((tm, d), lambda i, j: (i, 0)),
                  pl.BlockSpec((None, 1, d), lambda i, j: (layer, 0, 0)),
                  pl.BlockSpec((None, d, tn), lambda i, j: (layer, 0, j))],
        out_specs=pl.BlockSpec((tm, tn), lambda i, j: (i, j)),
        out_shape=jax.ShapeDtypeStruct((n, cols), BF16),
        scratch_shapes=[pltpu.VMEM((tm, d), BF16)],
        compiler_params=_params("arbitrary", "arbitrary"),
        name="norm_proj",
    )(x2, nw, w_all)


def _cumsum_rows(x, reverse):
    n = x.shape[0]
    row = lax.broadcasted_iota(jnp.int32, x.shape, 0)
    d = 1
    while d < n:
        if reverse:
            x = x + jnp.where(row < n - d, pltpu.roll(x, n - d, axis=0), 0.0)
        else:
            x = x + jnp.where(row >= d, pltpu.roll(x, d, axis=0), 0.0)
        d *= 2
    return x


def _softplus(x):
    return jnp.maximum(x, 0.0) + jnp.log1p(jnp.exp(-jnp.abs(x)))


def _ssd_kernel(rev, *refs):
    if rev:
        (prev_ref, main_ref, next_ref, dt_ref, cw_ref, cb_ref, dtb_ref, alog_ref, dsk_ref,
         z_ref, yf_ref, nw_ref, o_ref, ext_scr, h_scr) = refs
    else:
        (prev_ref, main_ref, next_ref, dt_ref, cw_ref, cb_ref, dtb_ref, alog_ref, dsk_ref,
         o_ref, ext_scr, h_scr) = refs
    c = pl.program_id(1)
    nc = pl.num_programs(1)
    cc = nc - 1 - c if rev else c

    @pl.when(c == 0)
    def _():
        h_scr[...] = jnp.zeros_like(h_scr)

    ext_scr[0:CONV_HALO, :] = jnp.where(cc == 0, 0.0, prev_ref[...].astype(F32))
    ext_scr[CONV_HALO:CONV_HALO + CHUNK, :] = main_ref[...].astype(F32)
    ext_scr[CONV_HALO + CHUNK:, :] = jnp.where(cc == nc - 1, 0.0, next_ref[...].astype(F32))
    acc = jnp.broadcast_to(cb_ref[...], (CHUNK, SSD_XBC))
    for k in range(SSD_CONV):
        acc = acc + cw_ref[k:k + 1, :] * ext_scr[pl.ds(CONV_HALO - SSD_CONV // 2 + k, CHUNK), :]
    act = _silu(acc)
    gn = SSD_GROUPS * SSD_STATE
    x = act[:, :SSD_WIDTH]
    b_mat = act[:, SSD_WIDTH:SSD_WIDTH + gn]
    c_mat = act[:, SSD_WIDTH + gn:]

    off = SSD_HEADS if rev else 0
    dt = _softplus(dt_ref[...].astype(F32) + dtb_ref[...])
    a = -jnp.exp(alog_ref[...])
    cs = _cumsum_rows(dt * a, rev)
    cs_t = cs.T
    rowi = lax.broadcasted_iota(jnp.int32, (CHUNK, CHUNK), 0)
    coli = lax.broadcasted_iota(jnp.int32, (CHUNK, CHUNK), 1)
    mask = (rowi < coli) if rev else (rowi >= coli)

    def expand(v):
        return jnp.concatenate(
            [jnp.broadcast_to(v[:, off + h:off + h + 1], (CHUNK, SSD_HEAD_DIM)) for h in range(SSD_HEADS)], axis=1)

    dt_e = expand(dt)
    cs_e = expand(cs)
    edge = cs_e[0:1, :] if rev else cs_e[CHUNK - 1:CHUNK, :]
    xdt = x * dt_e
    xst = (xdt * jnp.exp(edge - cs_e)).astype(BF16)
    xdt = xdt.astype(BF16)
    e_in = jnp.exp(cs_e)
    e_chunk = jnp.exp(edge)

    r = SSD_HEADS // SSD_GROUPS
    gw = r * SSD_HEAD_DIM
    ys = []
    for g in range(SSD_GROUPS):
        bg = b_mat[:, g * SSD_STATE:(g + 1) * SSD_STATE].astype(BF16)
        cg = c_mat[:, g * SSD_STATE:(g + 1) * SSD_STATE].astype(BF16)
        scores = lax.dot_general(cg, bg, (((1,), (1,)), ((), ())), preferred_element_type=F32)
        h_prev = h_scr[g]
        y_off = jnp.dot(cg, h_prev.astype(BF16), preferred_element_type=F32) * e_in[:, g * gw:(g + 1) * gw]
        st = lax.dot_general(bg, xst[:, g * gw:(g + 1) * gw], (((0,), (0,)), ((), ())),
                             preferred_element_type=F32)
        h_scr[g] = h_prev * e_chunk[:, g * gw:(g + 1) * gw] + st
        for j in range(r):
            h = g * r + j
            colb = jnp.broadcast_to(cs[:, off + h:off + h + 1], (CHUNK, CHUNK))
            rowb = cs_t[off + h:off + h + 1, :]
            decay = jnp.exp(jnp.where(mask, colb - rowb, -jnp.inf))
            m = (scores * decay).astype(BF16)
            y_d = jnp.dot(m, xdt[:, h * SSD_HEAD_DIM:(h + 1) * SSD_HEAD_DIM], preferred_element_type=F32)
            ys.append(y_d + y_off[:, j * SSD_HEAD_DIM:(j + 1) * SSD_HEAD_DIM])
    y = jnp.concatenate(ys, axis=1)

    if rev:
        y = (y + yf_ref[...]) * _silu(z_ref[...].astype(F32))
        ms = jnp.mean(y * y, axis=-1, keepdims=True)
        o_ref[...] = (y * lax.rsqrt(ms + EPS) * nw_ref[...]).astype(o_ref.dtype)
    else:
        o_ref[...] = y + x * dsk_ref[...]


def _ssd_mixer(proj, bsz, seq, conv_w, conv_b, dtb_row, alog_row, dsk_row, norm_w):
    n = bsz * seq
    nc = seq // CHUNK
    hb = CHUNK // CONV_HALO
    nhalo = n // CONV_HALO

    def call(rev, extra_in, extra_cols, out_dtype):
        chunk = (lambda c: nc - 1 - c) if rev else (lambda c: c)
        row = lambda b, c: b * nc + chunk(c)
        in_specs = [
            pl.BlockSpec((CONV_HALO, SSD_XBC), lambda b, c: (jnp.maximum(row(b, c) * hb - 1, 0), COL_XBC // 8)),
            pl.BlockSpec((CHUNK, SSD_XBC), lambda b, c: (row(b, c), COL_XBC // 8)),
            pl.BlockSpec((CONV_HALO, SSD_XBC), lambda b, c: (jnp.minimum((row(b, c) + 1) * hb, nhalo - 1), COL_XBC // 8)),
            pl.BlockSpec((CHUNK, LANES), lambda b, c: (row(b, c), COL_DT)),
            pl.BlockSpec((SSD_CONV, SSD_XBC), lambda b, c: (0, 0)),
            pl.BlockSpec((1, SSD_XBC), lambda b, c: (0, 0)),
            pl.BlockSpec((1, LANES), lambda b, c: (0, 0)),
            pl.BlockSpec((1, LANES), lambda b, c: (0, 0)),
            pl.BlockSpec((1, SSD_WIDTH), lambda b, c: (0, 0)),
        ] + [pl.BlockSpec((CHUNK, SSD_WIDTH), (lambda cb: (lambda b, c: (row(b, c), cb)))(cb)) for cb in extra_cols]
        if rev:
            in_specs.append(pl.BlockSpec((1, SSD_WIDTH), lambda b, c: (0, 0)))
        return pl.pallas_call(
            functools.partial(_ssd_kernel, rev),
            grid=(bsz, nc),
            in_specs=in_specs,
            out_specs=pl.BlockSpec((CHUNK, SSD_WIDTH), lambda b, c: (row(b, c), 0)),
            out_shape=jax.ShapeDtypeStruct((n, SSD_WIDTH), out_dtype),
            scratch_shapes=[pltpu.VMEM((CHUNK + 2 * CONV_HALO, SSD_XBC), F32),
                            pltpu.VMEM((SSD_GROUPS, SSD_STATE, SSD_WIDTH // SSD_GROUPS), F32)],
            compiler_params=_params("arbitrary", "arbitrary"),
            name="ssd_bwd" if rev else "ssd_fwd",
        )(proj, proj, proj, proj, conv_w, conv_b, dtb_row, alog_row, dsk_row, *extra_in)

    y_f = call(False, (), (), F32)
    return call(True, (proj, y_f, norm_w), (COL_Z // 4, 0), BF16)


def _rotary_tables(seq, head_dim):
    n_freq = head_dim // 4
    freqs = ROPE_THETA ** (-jnp.arange(n_freq, dtype=F32) / n_freq)
    t = jnp.arange(seq)
    ang_row = (t // GRID_W).astype(F32)[:, None] * freqs
    ang_col = (t % GRID_W).astype(F32)[:, None] * freqs
    cos = jnp.concatenate([jnp.cos(ang_row)] * 2 + [jnp.cos(ang_col)] * 2, axis=1)
    sin = jnp.concatenate([-jnp.sin(ang_row), jnp.sin(ang_row), -jnp.sin(ang_col), jnp.sin(ang_col)], axis=1)
    return cos, sin


def _rotate(x, cos, sin_signed, quarter):
    w = x.shape[1]
    lane = lax.broadcasted_iota(jnp.int32, x.shape, 1)
    partner = jnp.where((lane % (2 * quarter)) < quarter,
                        pltpu.roll(x, w - quarter, axis=1), pltpu.roll(x, quarter, axis=1))
    return x * cos + partner * sin_signed


def _ret_kernel(rev, *refs):
    if rev:
        (lg_ref, q_ref, k_ref, v_ref, cos_ref, sin_ref, g_ref, yf_ref, o_ref, tab_scr, h_scr) = refs
    else:
        (lg_ref, q_ref, k_ref, v_ref, cos_ref, sin_ref, o_ref, tab_scr, h_scr) = refs
    first = jnp.logical_and(pl.program_id(0) == 0, pl.program_id(1) == 0)

    @pl.when(first)
    def _():
        rowi = lax.broadcasted_iota(jnp.int32, (CHUNK, CHUNK), 0)
        coli = lax.broadcasted_iota(jnp.int32, (CHUNK, CHUNK), 1)
        rowf = rowi.astype(F32)
        for h in range(RET_HEADS):
            lg = lg_ref[(RET_HEADS if rev else 0) + h]
            if rev:
                dist = jnp.where(coli > rowi, (coli - rowi).astype(F32) * lg, -jnp.inf)
                kd = rowf * lg
                qd = (CHUNK - rowf) * lg
            else:
                dist = jnp.where(rowi >= coli, (rowi - coli).astype(F32) * lg, -jnp.inf)
                kd = (CHUNK - 1 - rowf) * lg
                qd = (rowf + 1.0) * lg
            tab_scr[0, h] = jnp.exp(dist)
            tab_scr[1, h] = jnp.exp(kd)
            tab_scr[2, h] = jnp.exp(qd)
            tab_scr[3, h] = jnp.exp(jnp.full((CHUNK, CHUNK), CHUNK, F32) * lg)

    @pl.when(pl.program_id(1) == 0)
    def _():
        h_scr[...] = jnp.zeros_like(h_scr)

    cos = jnp.concatenate([cos_ref[...]] * RET_HEADS, axis=1)
    sin = jnp.concatenate([sin_ref[...]] * RET_HEADS, axis=1)
    q = _rotate(q_ref[...].astype(F32), cos, sin, RET_DIM // 4)
    k = _rotate(k_ref[...].astype(F32), cos, sin, RET_DIM // 4) * (RET_DIM ** -0.5)
    v = v_ref[...].astype(F32)
    ys = []
    for h in range(RET_HEADS):
        sl = slice(h * RET_DIM, (h + 1) * RET_DIM)
        qh = q[:, sl].astype(BF16)
        kh = k[:, sl].astype(BF16)
        vh = v[:, sl]
        s = lax.dot_general(qh, kh, (((1,), (1,)), ((), ())), preferred_element_type=F32) * tab_scr[0, h]
        inner = jnp.dot(s.astype(BF16), vh.astype(BF16), preferred_element_type=F32)
        h_prev = h_scr[h]
        cross = jnp.dot(qh, h_prev.astype(BF16), preferred_element_type=F32) * tab_scr[2, h]
        st = lax.dot_general(kh, (vh * tab_scr[1, h]).astype(BF16), (((0,), (0,)), ((), ())),
                             preferred_element_type=F32)
        h_scr[h] = h_prev * tab_scr[3, h] + st
        ys.append(inner + cross)

    if rev:
        outs = []
        for h in range(RET_HEADS):
            sl = slice(h * RET_DIM, (h + 1) * RET_DIM)
            y = ys[h] + yf_ref[:, sl]
            y = y * lax.rsqrt(jnp.mean(y * y, axis=-1, keepdims=True) + EPS)
            outs.append(y * _silu(g_ref[:, sl].astype(F32)))
        o_ref[...] = jnp.concatenate(outs, axis=1).astype(o_ref.dtype)
    else:
        o_ref[...] = jnp.concatenate(ys, axis=1)


def _ret_mixer(proj, bsz, seq, log_gamma, cos, sin):
    n = bsz * seq
    nc = seq // CHUNK
    width = RET_HEADS * RET_DIM

    def call(rev, extra_in, extra_cols, out_dtype):
        chunk = (lambda c: nc - 1 - c) if rev else (lambda c: c)
        row = lambda b, c: b * nc + chunk(c)
        pcol = lambda cb: pl.BlockSpec((CHUNK, width), lambda b, c: (row(b, c), cb))
        tab = pl.BlockSpec((CHUNK, RET_DIM), lambda b, c: (chunk(c), 0))
        in_specs = [pl.BlockSpec(memory_space=pltpu.SMEM),
                    pcol(COL_RQ // 4), pcol(COL_RK // 4), pcol(COL_RV // 4), tab, tab] + [pcol(cb) for cb in extra_cols]
        return pl.pallas_call(
            functools.partial(_ret_kernel, rev),
            grid=(bsz, nc),
            in_specs=in_specs,
            out_specs=pl.BlockSpec((CHUNK, width), lambda b, c: (row(b, c), 0)),
            out_shape=jax.ShapeDtypeStruct((n, width), out_dtype),
            scratch_shapes=[pltpu.VMEM((4, RET_HEADS, CHUNK, CHUNK), F32),
                            pltpu.VMEM((RET_HEADS, RET_DIM, RET_DIM), F32)],
            compiler_params=_params("arbitrary", "arbitrary"),
            name="ret_bwd" if rev else "ret_fwd",
        )(log_gamma, proj, proj, proj, cos, sin, *extra_in)

    y_f = call(False, (), (), F32)
    return call(True, (proj, y_f), (COL_RG // 4, 0), BF16)


def _head_mean_square(x, ones_bd):
    x2 = x * x
    hi = x2.astype(BF16)
    lo = (x2 - hi.astype(F32)).astype(BF16)
    tot = jnp.dot(hi, ones_bd, preferred_element_type=F32) + jnp.dot(lo, ones_bd, preferred_element_type=F32)
    return tot * (1.0 / ATT_HEAD_DIM)


def _pad_heads(x, n_heads, fill=None):
    pad = jnp.zeros((x.shape[0], LANES - ATT_HEAD_DIM), x.dtype) if fill is None else fill
    parts = []
    for h in range(n_heads):
        parts += [x[:, h * ATT_HEAD_DIM:(h + 1) * ATT_HEAD_DIM], pad]
    return jnp.concatenate(parts, axis=1)


def _att_prep_kernel(q_ref, k_ref, v_ref, qn_ref, kn_ref, cos_ref, sin_ref, ones_ref, qo_ref, ko_ref, vo_ref):
    rep = ATT_HEADS // ATT_KV_HEADS
    cos1 = cos_ref[...]
    sin1 = sin_ref[...]
    q = q_ref[...].astype(F32)
    q = q * lax.rsqrt(_head_mean_square(q, ones_ref[...]) + EPS) * qn_ref[...]
    q = _rotate(q, jnp.concatenate([cos1] * rep, axis=1), jnp.concatenate([sin1] * rep, axis=1), ATT_HEAD_DIM // 4)
    qo_ref[...] = _pad_heads((q * (ATT_HEAD_DIM ** -0.5)).astype(qo_ref.dtype), ATT_HEADS)
    k = k_ref[...].astype(F32)
    kw = ATT_KV_HEADS * ATT_HEAD_DIM
    k = k * lax.rsqrt(_head_mean_square(k, ones_ref[0:kw, 0:kw]) + EPS) * kn_ref[...]
    k = _rotate(k, cos1, sin1, ATT_HEAD_DIM // 4)
    ko_ref[...] = _pad_heads(k.astype(ko_ref.dtype), ATT_KV_HEADS)
    v = v_ref[...]
    vo_ref[...] = _pad_heads(v, ATT_KV_HEADS, fill=jnp.ones((v.shape[0], LANES - ATT_HEAD_DIM), v.dtype))


def _flash_kernel(tq, tk, q_ref, k_ref, v_ref, o_ref, m_scr, acc_scr):
    ki = pl.program_id(2)
    rep = ATT_HEADS // ATT_KV_HEADS

    @pl.when(ki == 0)
    def _():
        m_scr[...] = jnp.full_like(m_scr, -jnp.inf)
        acc_scr[...] = jnp.zeros_like(acc_scr)

    for g in range(ATT_KV_HEADS):
        qs = jnp.concatenate([q_ref[:, (rep * g + j) * LANES:(rep * g + j + 1) * LANES] for j in range(rep)], axis=0)
        kg = k_ref[:, g * LANES:(g + 1) * LANES]
        vg = v_ref[:, g * LANES:(g + 1) * LANES]
        s = lax.dot_general(qs, kg, (((1,), (1,)), ((), ())), preferred_element_type=F32)
        m_prev = m_scr[g]
        m_new = jnp.maximum(m_prev, jnp.max(s, axis=1, keepdims=True))
        p = jnp.exp(s - jnp.tile(m_new, (1, tk // LANES)))
        alpha = jnp.exp(m_prev - m_new)
        acc_scr[g] = alpha * acc_scr[g] + jnp.dot(p.astype(BF16), vg, preferred_element_type=F32)
        m_scr[g] = m_new

    @pl.when(ki == pl.num_programs(2) - 1)
    def _():
        outs = []
        for g in range(ATT_KV_HEADS):
            for j in range(rep):
                rows = acc_scr[g, j * tq:(j + 1) * tq, :]
                outs.append(rows[:, :ATT_HEAD_DIM] / rows[:, ATT_HEAD_DIM:])
        o_ref[...] = jnp.concatenate(outs, axis=1).astype(o_ref.dtype)


def _att_mixer(proj, bsz, seq, q_norm, k_norm, cos, sin, ones_bd):
    n = bsz * seq
    qw = ATT_HEADS * ATT_HEAD_DIM
    kw = ATT_KV_HEADS * ATT_HEAD_DIM
    qpw = ATT_HEADS * LANES
    kpw = ATT_KV_HEADS * LANES
    tm = min(512, seq)
    nt = seq // tm
    qn_row = jnp.tile(q_norm, ATT_HEADS)[None, :]
    kn_row = jnp.tile(k_norm, ATT_KV_HEADS)[None, :]
    q_pad, k_pad, v_ext = pl.pallas_call(
        _att_prep_kernel,
        grid=(bsz, nt),
        in_specs=[pl.BlockSpec((tm, qw), lambda b, i: (b * nt + i, COL_AQ // 4)),
                  pl.BlockSpec((tm, kw), lambda b, i: (b * nt + i, COL_AK)),
                  pl.BlockSpec((tm, kw), lambda b, i: (b * nt + i, COL_AV)),
                  pl.BlockSpec((1, qw), lambda b, i: (0, 0)),
                  pl.BlockSpec((1, kw), lambda b, i: (0, 0)),
                  pl.BlockSpec((tm, kw), lambda b, i: (i, 0)),
                  pl.BlockSpec((tm, kw), lambda b, i: (i, 0)),
                  pl.BlockSpec((qw, qw), lambda b, i: (0, 0))],
        out_specs=[pl.BlockSpec((tm, qpw), lambda b, i: (b * nt + i, 0)),
                   pl.BlockSpec((tm, kpw), lambda b, i: (b * nt + i, 0)),
                   pl.BlockSpec((tm, kpw), lambda b, i: (b * nt + i, 0))],
        out_shape=[jax.ShapeDtypeStruct((n, qpw), BF16), jax.ShapeDtypeStruct((n, kpw), BF16),
                   jax.ShapeDtypeStruct((n, kpw), BF16)],
        compiler_params=_params("arbitrary", "arbitrary"),
        name="att_prep",
    )(proj, proj, proj, qn_row, kn_row, cos, sin, ones_bd)

    tq = min(512, seq)
    tk = min(1024, seq)
    nq = seq // tq
    nk = seq // tk
    rep = ATT_HEADS // ATT_KV_HEADS
    return pl.pallas_call(
        functools.partial(_flash_kernel, tq, tk),
        grid=(bsz, nq, nk),
        in_specs=[pl.BlockSpec((tq, qpw), lambda b, i, j: (b * nq + i, 0)),
                  pl.BlockSpec((tk, kpw), lambda b, i, j: (b * nk + j, 0)),
                  pl.BlockSpec((tk, kpw), lambda b, i, j: (b * nk + j, 0))],
        out_specs=pl.BlockSpec((tq, qw), lambda b, i, j: (b * nq + i, 0)),
        out_shape=jax.ShapeDtypeStruct((n, qw), BF16),
        scratch_shapes=[pltpu.VMEM((ATT_KV_HEADS, rep * tq, LANES), F32),
                        pltpu.VMEM((ATT_KV_HEADS, rep * tq, LANES), F32)],
        compiler_params=_params("arbitrary", "arbitrary", "arbitrary"),
        name="flash_attention",
    )(q_pad, k_pad, v_ext)


def _fft_split(seq):
    l1 = 1 << (int(math.log2(seq)) // 2)
    return l1, seq // l1


def _fft_tables(seq):
    l1, l2 = _fft_split(seq)
    c = np.arange(FNET_GROUP_DIM)
    ang = 2.0 * np.pi * ((c[:, None] * c[None, :]) % FNET_GROUP_DIM) / FNET_GROUP_DIM
    chan = np.concatenate([np.cos(ang), -np.sin(ang)], axis=1)
    k1 = np.arange(l1)
    ang1 = 2.0 * np.pi * ((k1[:, None] * k1[None, :]) % l1) / l1
    t2 = np.arange(l2)
    k2 = np.arange(l2)
    kk = k1[:, None, None] + l1 * k2[None, :, None]
    ang2 = 2.0 * np.pi * ((kk * t2[None, None, :]) % seq) / seq
    stage2 = np.concatenate([np.cos(ang2), np.sin(ang2)], axis=2)
    to = lambda a: jnp.asarray(a, dtype=BF16)
    return to(chan), to(np.cos(ang1)), to(np.sin(ang1)), to(stage2)


FFT_ROWS = 8


def _fft_chan_kernel(f_ref, chan_ref, zr_ref, zi_ref):
    zr, zi = [], []
    for g in range(FNET_GROUPS):
        z = jnp.dot(f_ref[:, g * FNET_GROUP_DIM:(g + 1) * FNET_GROUP_DIM], chan_ref[...],
                    preferred_element_type=F32)
        zr.append(z[:, :FNET_GROUP_DIM])
        zi.append(z[:, FNET_GROUP_DIM:])
    zr_ref[...] = jnp.concatenate(zr, axis=1)
    zi_ref[...] = jnp.concatenate(zi, axis=1)


def _fft_a_kernel(zr_ref, zi_ref, c1_ref, s1_ref, ur_ref, ui_ref):
    c1 = c1_ref[...]
    s1 = s1_ref[...]
    dot = functools.partial(jnp.dot, preferred_element_type=F32)
    for j in range(FFT_ROWS):
        zr = zr_ref[:, j, :].astype(BF16)
        zi = zi_ref[:, j, :].astype(BF16)
        ur_ref[:, j, :] = dot(c1, zr) + dot(s1, zi)
        ui_ref[:, j, :] = dot(c1, zi) - dot(s1, zr)


def _fft_b_kernel(scale, ur_ref, ui_ref, m_ref, o_ref):
    for j in range(FFT_ROWS):
        u = jnp.concatenate([ur_ref[j], ui_ref[j]], axis=0).astype(BF16)
        o_ref[:, j, :] = jnp.dot(m_ref[j], u, preferred_element_type=F32) * scale


def _fft_mixer(proj, bsz, seq, tables):
    chan, c1, s1, stage2 = tables
    l1, l2 = _fft_split(seq)
    n = bsz * seq
    w = FNET_GROUPS * FNET_GROUP_DIM
    tm = min(1024, n)
    z_r, z_i = pl.pallas_call(
        _fft_chan_kernel,
        grid=(n // tm,),
        in_specs=[pl.BlockSpec((tm, w), lambda i: (i, COL_F // 4)),
                  pl.BlockSpec((FNET_GROUP_DIM, 2 * FNET_GROUP_DIM), lambda i: (0, 0))],
        out_specs=[pl.BlockSpec((tm, w), lambda i: (i, 0))] * 2,
        out_shape=[jax.ShapeDtypeStruct((n, w), F32)] * 2,
        compiler_params=_params("arbitrary"),
        name="fft_channels",
    )(proj, chan)
    slab = pl.BlockSpec((None, l1, FFT_ROWS, w), lambda b, t: (b, 0, t, 0))
    u_r, u_i = pl.pallas_call(
        _fft_a_kernel,
        grid=(bsz, l2 // FFT_ROWS),
        in_specs=[slab, slab, pl.BlockSpec((l1, l1), lambda b, t: (0, 0)), pl.BlockSpec((l1, l1), lambda b, t: (0, 0))],
        out_specs=[slab, slab],
        out_shape=[jax.ShapeDtypeStruct((bsz, l1, l2, w), F32)] * 2,
        compiler_params=_params("arbitrary", "arbitrary"),
        name="fft_stage_a",
    )(z_r.reshape(bsz, l1, l2, w), z_i.reshape(bsz, l1, l2, w), c1, s1)
    scale = 1.0 / math.sqrt(seq * FNET_GROUP_DIM)
    rows = pl.BlockSpec((None, FFT_ROWS, l2, w), lambda b, i: (b, i, 0, 0))
    y = pl.pallas_call(
        functools.partial(_fft_b_kernel, scale),
        grid=(bsz, l1 // FFT_ROWS),
        in_specs=[rows, rows, pl.BlockSpec((FFT_ROWS, l2, 2 * l2), lambda b, i: (i, 0, 0))],
        out_specs=pl.BlockSpec((None, l2, FFT_ROWS, w), lambda b, i: (b, 0, i, 0)),
        out_shape=jax.ShapeDtypeStruct((bsz, l2, l1, w), F32),
        compiler_params=_params("arbitrary", "arbitrary"),
        name="fft_stage_b",
    )(u_r, u_i, stage2)
    return y.reshape(n, w)


def _merge_kernel(x_ref, g0_ref, g1_ref, g2_ref, g3_ref, o0_ref, o1_ref, o2_ref, o3_ref,
                  wb_ref, wo_ref, nf_ref, wr_ref, xo_ref, u_ref, aff_ref):
    merged = None
    for kbr, (g_ref, o_ref) in enumerate(((g0_ref, o0_ref), (g1_ref, o1_ref), (g2_ref, o2_ref), (g3_ref, o3_ref))):
        term = jax.nn.sigmoid(g_ref[...].astype(F32)) * jnp.dot(o_ref[...].astype(BF16), wb_ref[kbr], preferred_element_type=F32)
        merged = term if merged is None else merged + term
    x = x_ref[...] + jnp.dot(merged.astype(BF16), wo_ref[...], preferred_element_type=F32)
    xo_ref[...] = x
    ms = jnp.mean(x * x, axis=-1, keepdims=True)
    u = (x * lax.rsqrt(ms + EPS) * nf_ref[...]).astype(BF16)
    u_ref[...] = u
    logits = jnp.dot(u, wr_ref[...], preferred_element_type=F32)
    lane = lax.broadcasted_iota(jnp.int32, logits.shape, 1)
    logits = jnp.where(lane < N_EXPERTS, logits, -jnp.inf)
    e = jnp.exp(logits - jnp.max(logits, axis=-1, keepdims=True))
    aff_ref[...] = e / jnp.sum(e, axis=-1, keepdims=True)


def _merge(x2, proj, branches, w_branch, w_out, norm_ffn, w_router, layer):
    n, d = x2.shape
    tm = min(256, n)
    bw = SSD_WIDTH
    gate_spec = lambda kbr: pl.BlockSpec((tm, d), lambda i: (i, COL_GATES // 8 + kbr))
    br_spec = pl.BlockSpec((tm, bw), lambda i: (i, 0))
    return pl.pallas_call(
        _merge_kernel,
        grid=(n // tm,),
        in_specs=[pl.BlockSpec((tm, d), lambda i: (i, 0))] + [gate_spec(k) for k in range(N_BRANCHES)]
                 + [br_spec] * N_BRANCHES
                 + [pl.BlockSpec((None, N_BRANCHES, bw, d), lambda i: (layer, 0, 0, 0)),
                    pl.BlockSpec((None, d, d), lambda i: (layer, 0, 0)),
                    pl.BlockSpec((None, 1, d), lambda i: (layer, 0, 0)),
                    pl.BlockSpec((None, d, LANES), lambda i: (layer, 0, 0))],
        out_specs=[pl.BlockSpec((tm, d), lambda i: (i, 0)), pl.BlockSpec((tm, d), lambda i: (i, 0)),
                   pl.BlockSpec((tm, LANES), lambda i: (i, 0))],
        out_shape=[jax.ShapeDtypeStruct((n, d), F32), jax.ShapeDtypeStruct((n, d), BF16),
                   jax.ShapeDtypeStruct((n, LANES), F32)],
        compiler_params=_params("arbitrary"),
        name="merge_out",
    )(x2, proj, proj, proj, proj, *branches, w_branch, w_out, norm_ffn, w_router)


def _ffn_kernel(xe_ref, gate_ref, wu_ref, wd_ref, o_ref):
    h = jnp.dot(xe_ref[...], wu_ref[...], preferred_element_type=F32)
    act = (_silu(h[:, :EXPERT_FF]) * h[:, EXPERT_FF:]).astype(BF16)
    o_ref[...] = jnp.dot(act, wd_ref[...], preferred_element_type=F32) * gate_ref[...]


def _expert_ffn(xe, gate, w_up, w_down, layer):
    e, cap, d = xe.shape
    tc = min(512, cap)
    return pl.pallas_call(
        _ffn_kernel,
        grid=(e, cap // tc),
        in_specs=[pl.BlockSpec((None, tc, d), lambda i, j: (i, j, 0)),
                  pl.BlockSpec((None, tc, 1), lambda i, j: (i, j, 0)),
                  pl.BlockSpec((None, None, d, 2 * EXPERT_FF), lambda i, j: (layer, i, 0, 0)),
                  pl.BlockSpec((None, None, EXPERT_FF, d), lambda i, j: (layer, i, 0, 0))],
        out_specs=pl.BlockSpec((None, tc, d), lambda i, j: (i, j, 0)),
        out_shape=jax.ShapeDtypeStruct((e, cap, d), F32),
        compiler_params=_params("arbitrary", "arbitrary"),
        name="expert_ffn",
    )(xe, gate, w_up, w_down)


def _norm_kernel(x_ref, w_ref, o_ref):
    x = x_ref[...]
    ms = jnp.mean(x * x, axis=-1, keepdims=True)
    o_ref[...] = x * lax.rsqrt(ms + EPS) * w_ref[...]


def _final_norm(x2, w):
    n, d = x2.shape
    tm = min(1024, n)
    return pl.pallas_call(
        _norm_kernel,
        grid=(n // tm,),
        in_specs=[pl.BlockSpec((tm, d), lambda i: (i, 0)), pl.BlockSpec((1, d), lambda i: (0, 0))],
        out_specs=pl.BlockSpec((tm, d), lambda i: (i, 0)),
        out_shape=jax.ShapeDtypeStruct((n, d), F32),
        compiler_params=_params("arbitrary"),
        name="final_norm",
    )(x2, w)


def _pad_lanes(v):
    return jnp.pad(v, (0, LANES - v.shape[0]))[None, :]


def _prepare(norm_mix, w_in, conv_w, conv_b, ssd_dt_bias, ssd_a_log, ssd_d, ssd_norm, ret_log2_decay,
             attn_q_norm, attn_k_norm, w_branch, w_out, norm_ffn, w_router, w_up, w_down, norm_final):
    w_pad = jnp.concatenate([w_in, jnp.zeros((DEPTH, D_MODEL, 1), w_in.dtype)], axis=-1)
    p = dict(
        norm_mix=norm_mix[:, None, :],
        w_proj=jnp.take(w_pad, jnp.asarray(_PROJ_PERM), axis=-1).astype(BF16),
        conv_w=conv_w, conv_b=conv_b[:, None, :],
        dtb=[_pad_lanes(ssd_dt_bias[l].reshape(-1)) for l in range(DEPTH)],
        alog=[_pad_lanes(ssd_a_log[l].reshape(-1)) for l in range(DEPTH)],
        dsk=[jnp.repeat(ssd_d[l], SSD_HEAD_DIM)[None, :] for l in range(DEPTH)],
        ssd_norm=ssd_norm[:, None, :],
        log_gamma=jnp.log1p(-jnp.exp2(ret_log2_decay.astype(F32))).reshape(DEPTH, 2 * RET_HEADS),
        q_norm=attn_q_norm, k_norm=attn_k_norm,
        w_branch=w_branch.astype(BF16), w_out=w_out.astype(BF16),
        norm_ffn=norm_ffn[:, None, :],
        w_router=jnp.pad(w_router, ((0, 0), (0, 0), (0, LANES - N_EXPERTS))).astype(BF16),
        w_up=w_up.astype(BF16), w_down=w_down.astype(BF16),
        norm_final=norm_final[None, :],
    )
    qw = ATT_HEADS * ATT_HEAD_DIM
    hid = np.arange(qw) // ATT_HEAD_DIM
    p["ones_bd"] = jnp.asarray(hid[:, None] == hid[None, :], dtype=BF16)
    return p


def _trunk(x, p):
    bsz, seq, d = x.shape
    n = bsz * seq
    cap = (EC_CAPACITY_FACTOR * n) // N_EXPERTS
    ret_cos, ret_sin = _rotary_tables(seq, RET_DIM)
    att_cos, att_sin = [jnp.tile(t, (1, ATT_KV_HEADS)) for t in _rotary_tables(seq, ATT_HEAD_DIM)]
    fft_tables = _fft_tables(seq)
    x2 = x.reshape(n, d)
    for l in range(DEPTH):
        proj = _norm_proj(x2, p["norm_mix"], p["w_proj"], l)
        o_ssd = _ssd_mixer(proj, bsz, seq, p["conv_w"][l], p["conv_b"][l], p["dtb"][l], p["alog"][l], p["dsk"][l],
                           p["ssd_norm"][l])
        o_ret = _ret_mixer(proj, bsz, seq, p["log_gamma"][l], ret_cos, ret_sin)
        o_att = _att_mixer(proj, bsz, seq, p["q_norm"][l], p["k_norm"][l], att_cos, att_sin, p["ones_bd"])
        o_fft = _fft_mixer(proj, bsz, seq, fft_tables)
        x2, u, aff = _merge(x2, proj, (o_ssd, o_ret, o_att, o_fft), p["w_branch"], p["w_out"], p["norm_ffn"],
                            p["w_router"], l)
        gate, idx = lax.top_k(aff[:, :N_EXPERTS].T, cap)
        xe = jnp.take(u, idx, axis=0)
        ye = _expert_ffn(xe, gate[..., None], p["w_up"], p["w_down"], l)
        x2 = x2.at[idx.reshape(-1)].add(ye.reshape(-1, d))
    return _final_norm(x2, p["norm_final"]).reshape(bsz, seq, d)


def kernel(x_prompt, x_sample, norm_mix, w_in, conv_w, conv_b, ssd_dt_bias, ssd_a_log, ssd_d, ssd_norm, ret_log2_decay, attn_q_norm, attn_k_norm, w_branch, w_out, norm_ffn, w_router, w_up, w_down, norm_final):
    p = _prepare(norm_mix, w_in, conv_w, conv_b, ssd_dt_bias, ssd_a_log, ssd_d, ssd_norm, ret_log2_decay,
                 attn_q_norm, attn_k_norm, w_branch, w_out, norm_ffn, w_router, w_up, w_down, norm_final)
    return (_trunk(x_prompt, p), _trunk(x_sample, p))
```

```python
import functools
import math

import numpy as np
import jax
import jax.numpy as jnp
from jax import lax
from jax.experimental import pallas as pl
from jax.experimental.pallas import tpu as pltpu

F32 = jnp.float32
BF16 = jnp.bfloat16

D_MODEL = 1024
DEPTH = 4
GRID_W = 64
CHUNK = 128
EPS = 1e-6
ROPE_THETA = 10000.0

SSD_HEADS = 8
SSD_HEAD_DIM = 64
SSD_WIDTH = 512
SSD_GROUPS = 2
SSD_STATE = 128
SSD_CONV = 5
SSD_XBC = 1024

RET_HEADS = 4
RET_DIM = 128

ATT_HEADS = 8
ATT_KV_HEADS = 2
ATT_HEAD_DIM = 64

FNET_GROUPS = 4
FNET_GROUP_DIM = 128

N_BRANCHES = 4
N_EXPERTS = 16
EXPERT_FF = 1024
EC_CAPACITY_FACTOR = 2

IN_SIZES = (512, 1024, 8, 8, 512, 512, 512, 512, 512, 128, 128, 512, 4096)
D_IN = sum(IN_SIZES)

LANES = 128
CONV_HALO = 16

COL_XBC, COL_Z, COL_RQ, COL_RK, COL_RV, COL_RG, COL_AQ = 0, 8, 12, 16, 20, 24, 28
COL_GATES, COL_F, COL_AK, COL_AV, COL_DT = 32, 64, 68, 69, 70
PROJ_COLS = 72 * LANES


def _repack_proj_weights(w_in):
    offs = np.concatenate([[0], np.cumsum(IN_SIZES)])
    (z, xbc, dtf, dtb, rq, rk, rv, rg, aq, ak, av, f, gates) = [w_in[..., offs[i]:offs[i + 1]] for i in range(13)]
    pad = jnp.zeros(w_in.shape[:-1] + (2 * LANES - 16,), w_in.dtype)
    w = jnp.concatenate([xbc, z, rq, rk, rv, rg, aq, gates, f, ak, av, dtf, dtb, pad], axis=-1)
    assert w.shape[-1] == PROJ_COLS
    return w.astype(BF16)


def _params(*sem):
    return pltpu.CompilerParams(dimension_semantics=sem, vmem_limit_bytes=56 * 1024 * 1024)


def _silu(x):
    return x * jax.nn.sigmoid(x)


def _proj_kernel(x_ref, nw_ref, w_ref, o_ref, u_scr):
    @pl.when(pl.program_id(1) == 0)
    def _():
        x = x_ref[...]
        ms = jnp.mean(x * x, axis=-1, keepdims=True)
        u_scr[...] = (x * lax.rsqrt(ms + EPS) * nw_ref[...]).astype(BF16)

    o_ref[...] = jnp.dot(u_scr[...], w_ref[...], preferred_element_type=F32).astype(o_ref.dtype)


def _norm_proj(x2, nw, w_all, layer):
    n, d = x2.shape
    cols = w_all.shape[-1]
    tm = min(1024, n)
    tn = 1536
    return pl.pallas_call(
        _proj_kernel,
        grid=(n // tm, cols // tn),
        in_specs=[pl.BlockSpec((tm, d), lambda i, j: (i, 0)),
                  pl.BlockSpec((None, 1, d), lambda i, j: (layer, 0, 0)),
                  pl.BlockSpec((None, d, tn), lambda i, j: (layer, 0, j))],
        out_specs=pl.BlockSpec((tm, tn), lambda i, j: (i, j)),
        out_shape=jax.ShapeDtypeStruct((n, cols), BF16),
        scratch_shapes=[pltpu.VMEM((tm, d), BF16)],
        compiler_params=_params("arbitrary", "arbitrary"),
        name="norm_proj",
    )(x2, nw, w_all)


def _cumsum_rows(x, reverse):
    n = x.shape[0]
    row = lax.broadcasted_iota(jnp.int32, x.shape, 0)
    d = 1
    while d < n:
        if reverse:
            x = x + jnp.where(row < n - d, pltpu.roll(x, n - d, axis=0), 0.0)
        else:
            x = x + jnp.where(row >= d, pltpu.roll(x, d, axis=0), 0.0)
        d *= 2
    return x


def _softplus(x):
    return jnp.maximum(x, 0.0) + jnp.log1p(jnp.exp(-jnp.abs(x)))


def _ssd_kernel(rev, *refs):
    if rev:
        (prev_ref, main_ref, next_ref, dt_ref, cw_ref, cb_ref, dtb_ref, alog_ref, dsk_ref,
         z_ref, yf_ref, nw_ref, o_ref, ext_scr, h_scr) = refs
    else:
        (prev_ref, main_ref, next_ref, dt_ref, cw_ref, cb_ref, dtb_ref, alog_ref, dsk_ref,
         o_ref, ext_scr, h_scr) = refs
    c = pl.program_id(1)
    nc = pl.num_programs(1)
    cc = nc - 1 - c if rev else c

    @pl.when(c == 0)
    def _():
        h_scr[...] = jnp.zeros_like(h_scr)

    ext_scr[0:CONV_HALO, :] = jnp.where(cc == 0, 0.0, prev_ref[...].astype(F32))
    ext_scr[CONV_HALO:CONV_HALO + CHUNK, :] = main_ref[...].astype(F32)
    ext_scr[CONV_HALO + CHUNK:, :] = jnp.where(cc == nc - 1, 0.0, next_ref[...].astype(F32))
    acc = jnp.broadcast_to(cb_ref[...], (CHUNK, SSD_XBC))
    for k in range(SSD_CONV):
        acc = acc + cw_ref[k:k + 1, :] * ext_scr[pl.ds(CONV_HALO - SSD_CONV // 2 + k, CHUNK), :]
    act = _silu(acc)
    gn = SSD_GROUPS * SSD_STATE
    x = act[:, :SSD_WIDTH]
    b_mat = act[:, SSD_WIDTH:SSD_WIDTH + gn]
    c_mat = act[:, SSD_WIDTH + gn:]

    off = SSD_HEADS if rev else 0
    dt = _softplus(dt_ref[...].astype(F32) + dtb_ref[...])
    a = -jnp.exp(alog_ref[...])
    cs = _cumsum_rows(dt * a, rev)
    cs_t = cs.T
    rowi = lax.broadcasted_iota(jnp.int32, (CHUNK, CHUNK), 0)
    coli = lax.broadcasted_iota(jnp.int32, (CHUNK, CHUNK), 1)
    mask = (rowi < coli) if rev else (rowi >= coli)

    def expand(v):
        return jnp.concatenate(
            [jnp.broadcast_to(v[:, off + h:off + h + 1], (CHUNK, SSD_HEAD_DIM)) for h in range(SSD_HEADS)], axis=1)

    dt_e = expand(dt)
    cs_e = expand(cs)
    edge = cs_e[0:1, :] if rev else cs_e[CHUNK - 1:CHUNK, :]
    xdt = x * dt_e
    xst = (xdt * jnp.exp(edge - cs_e)).astype(BF16)
    xdt = xdt.astype(BF16)
    e_in = jnp.exp(cs_e)
    e_chunk = jnp.exp(edge)

    r = SSD_HEADS // SSD_GROUPS
    gw = r * SSD_HEAD_DIM
    ys = []
    for g in range(SSD_GROUPS):
        bg = b_mat[:, g * SSD_STATE:(g + 1) * SSD_STATE].astype(BF16)
        cg = c_mat[:, g * SSD_STATE:(g + 1) * SSD_STATE].astype(BF16)
        scores = lax.dot_general(cg, bg, (((1,), (1,)), ((), ())), preferred_element_type=F32)
        h_prev = h_scr[g]
        y_off = jnp.dot(cg, h_prev.astype(BF16), preferred_element_type=F32) * e_in[:, g * gw:(g + 1) * gw]
        st = lax.dot_general(bg, xst[:, g * gw:(g + 1) * gw], (((0,), (0,)), ((), ())),
                             preferred_element_type=F32)
        h_scr[g] = h_prev * e_chunk[:, g * gw:(g + 1) * gw] + st
        for j in range(r):
            h = g * r + j
            colb = jnp.broadcast_to(cs[:, off + h:off + h + 1], (CHUNK, CHUNK))
            rowb = cs_t[off + h:off + h + 1, :]
            decay = jnp.exp(jnp.where(mask, colb - rowb, -jnp.inf))
            m = (scores * decay).astype(BF16)
            y_d = jnp.dot(m, xdt[:, h * SSD_HEAD_DIM:(h + 1) * SSD_HEAD_DIM], preferred_element_type=F32)
            ys.append(y_d + y_off[:, j * SSD_HEAD_DIM:(j + 1) * SSD_HEAD_DIM])
    y = jnp.concatenate(ys, axis=1)

    if rev:
        y = (y + yf_ref[...]) * _silu(z_ref[...].astype(F32))
        ms = jnp.mean(y * y, axis=-1, keepdims=True)
        o_ref[...] = (y * lax.rsqrt(ms + EPS) * nw_ref[...]).astype(o_ref.dtype)
    else:
        o_ref[...] = y + x * dsk_ref[...]


def _ssd_mixer(proj, bsz, seq, conv_w, conv_b, dtb_row, alog_row, dsk_row, norm_w):
    n = bsz * seq
    nc = seq // CHUNK
    hb = CHUNK // CONV_HALO
    nhalo = n // CONV_HALO

    def call(rev, extra_in, extra_cols, out_dtype):
        chunk = (lambda c: nc - 1 - c) if rev else (lambda c: c)
        row = lambda b, c: b * nc + chunk(c)
        in_specs = [
            pl.BlockSpec((CONV_HALO, SSD_XBC), lambda b, c: (jnp.maximum(row(b, c) * hb - 1, 0), COL_XBC // 8)),
            pl.BlockSpec((CHUNK, SSD_XBC), lambda b, c: (row(b, c), COL_XBC // 8)),
            pl.BlockSpec((CONV_HALO, SSD_XBC), lambda b, c: (jnp.minimum((row(b, c) + 1) * hb, nhalo - 1), COL_XBC // 8)),
            pl.BlockSpec((CHUNK, LANES), lambda b, c: (row(b, c), COL_DT)),
            pl.BlockSpec((SSD_CONV, SSD_XBC), lambda b, c: (0, 0)),
            pl.BlockSpec((1, SSD_XBC), lambda b, c: (0, 0)),
            pl.BlockSpec((1, LANES), lambda b, c: (0, 0)),
            pl.BlockSpec((1, LANES), lambda b, c: (0, 0)),
            pl.BlockSpec((1, SSD_WIDTH), lambda b, c: (0, 0)),
        ] + [pl.BlockSpec((CHUNK, SSD_WIDTH), (lambda cb: (lambda b, c: (row(b, c), cb)))(cb)) for cb in extra_cols]
        if rev:
            in_specs.append(pl.BlockSpec((1, SSD_WIDTH), lambda b, c: (0, 0)))
        return pl.pallas_call(
            functools.partial(_ssd_kernel, rev),
            grid=(bsz, nc),
            in_specs=in_specs,
            out_specs=pl.BlockSpec((CHUNK, SSD_WIDTH), lambda b, c: (row(b, c), 0)),
            out_shape=jax.ShapeDtypeStruct((n, SSD_WIDTH), out_dtype),
            scratch_shapes=[pltpu.VMEM((CHUNK + 2 * CONV_HALO, SSD_XBC), F32),
                            pltpu.VMEM((SSD_GROUPS, SSD_STATE, SSD_WIDTH // SSD_GROUPS), F32)],
            compiler_params=_params("arbitrary", "arbitrary"),
            name="ssd_bwd" if rev else "ssd_fwd",
        )(proj, proj, proj, proj, conv_w, conv_b, dtb_row, alog_row, dsk_row, *extra_in)

    y_f = call(False, (), (), F32)
    return call(True, (proj, y_f, norm_w), (COL_Z // 4, 0), BF16)


def _rotary_tables(seq, head_dim):
    n_freq = head_dim // 4
    freqs = ROPE_THETA ** (-jnp.arange(n_freq, dtype=F32) / n_freq)
    t = jnp.arange(seq)
    ang_row = (t // GRID_W).astype(F32)[:, None] * freqs
    ang_col = (t % GRID_W).astype(F32)[:, None] * freqs
    cos = jnp.concatenate([jnp.cos(ang_row)] * 2 + [jnp.cos(ang_col)] * 2, axis=1)
    sin = jnp.concatenate([-jnp.sin(ang_row), jnp.sin(ang_row), -jnp.sin(ang_col), jnp.sin(ang_col)], axis=1)
    return cos, sin


def _rotate(x, cos, sin_signed, quarter):
    w = x.shape[1]
    lane = lax.broadcasted_iota(jnp.int32, x.shape, 1)
    partner = jnp.where((lane % (2 * quarter)) < quarter,
                        pltpu.roll(x, w - quarter, axis=1), pltpu.roll(x, quarter, axis=1))
    return x * cos + partner * sin_signed


def _ret_kernel(rev, *refs):
    if rev:
        (lg_ref, q_ref, k_ref, v_ref, cos_ref, sin_ref, g_ref, yf_ref, o_ref, tab_scr, h_scr) = refs
    else:
        (lg_ref, q_ref, k_ref, v_ref, cos_ref, sin_ref, o_ref, tab_scr, h_scr) = refs
    first = jnp.logical_and(pl.program_id(0) == 0, pl.program_id(1) == 0)

    @pl.when(first)
    def _():
        rowi = lax.broadcasted_iota(jnp.int32, (CHUNK, CHUNK), 0)
        coli = lax.broadcasted_iota(jnp.int32, (CHUNK, CHUNK), 1)
        rowf = rowi.astype(F32)
        for h in range(RET_HEADS):
            lg = lg_ref[(RET_HEADS if rev else 0) + h]
            if rev:
                dist = jnp.where(coli > rowi, (coli - rowi).astype(F32) * lg, -jnp.inf)
                kd = rowf * lg
                qd = (CHUNK - rowf) * lg
            else:
                dist = jnp.where(rowi >= coli, (rowi - coli).astype(F32) * lg, -jnp.inf)
                kd = (CHUNK - 1 - rowf) * lg
                qd = (rowf + 1.0) * lg
            tab_scr[0, h] = jnp.exp(dist)
            tab_scr[1, h] = jnp.exp(kd)
            tab_scr[2, h] = jnp.exp(qd)
            tab_scr[3, h] = jnp.exp(jnp.full((CHUNK, CHUNK), CHUNK, F32) * lg)

    @pl.when(pl.program_id(1) == 0)
    def _():
        h_scr[...] = jnp.zeros_like(h_scr)

    cos = jnp.concatenate([cos_ref[...]] * RET_HEADS, axis=1)
    sin = jnp.concatenate([sin_ref[...]] * RET_HEADS, axis=1)
    q = _rotate(q_ref[...].astype(F32), cos, sin, RET_DIM // 4)
    k = _rotate(k_ref[...].astype(F32), cos, sin, RET_DIM // 4) * (RET_DIM ** -0.5)
    v = v_ref[...].astype(F32)
    ys = []
    for h in range(RET_HEADS):
        sl = slice(h * RET_DIM, (h + 1) * RET_DIM)
        qh = q[:, sl].astype(BF16)
        kh = k[:, sl].astype(BF16)
        vh = v[:, sl]
        s = lax.dot_general(qh, kh, (((1,), (1,)), ((), ())), preferred_element_type=F32) * tab_scr[0, h]
        inner = jnp.dot(s.astype(BF16), vh.astype(BF16), preferred_element_type=F32)
        h_prev = h_scr[h]
        cross = jnp.dot(qh, h_prev.astype(BF16), preferred_element_type=F32) * tab_scr[2, h]
        st = lax.dot_general(kh, (vh * tab_scr[1, h]).astype(BF16), (((0,), (0,)), ((), ())),
                             preferred_element_type=F32)
        h_scr[h] = h_prev * tab_scr[3, h] + st
        ys.append(inner + cross)

    if rev:
        outs = []
        for h in range(RET_HEADS):
            sl = slice(h * RET_DIM, (h + 1) * RET_DIM)
            y = ys[h] + yf_ref[:, sl]
            y = y * lax.rsqrt(jnp.mean(y * y, axis=-1, keepdims=True) + EPS)
            outs.append(y * _silu(g_ref[:, sl].astype(F32)))
        o_ref[...] = jnp.concatenate(outs, axis=1).astype(o_ref.dtype)
    else:
        o_ref[...] = jnp.concatenate(ys, axis=1)


def _ret_mixer(proj, bsz, seq, log_gamma, cos, sin):
    n = bsz * seq
    nc = seq // CHUNK
    width = RET_HEADS * RET_DIM

    def call(rev, extra_in, extra_cols, out_dtype):
        chunk = (lambda c: nc - 1 - c) if rev else (lambda c: c)
        row = lambda b, c: b * nc + chunk(c)
        pcol = lambda cb: pl.BlockSpec((CHUNK, width), lambda b, c: (row(b, c), cb))
        tab = pl.BlockSpec((CHUNK, RET_DIM), lambda b, c: (chunk(c), 0))
        in_specs = [pl.BlockSpec(memory_space=pltpu.SMEM),
                    pcol(COL_RQ // 4), pcol(COL_RK // 4), pcol(COL_RV // 4), tab, tab] + [pcol(cb) for cb in extra_cols]
        return pl.pallas_call(
            functools.partial(_ret_kernel, rev),
            grid=(bsz, nc),
            in_specs=in_specs,
            out_specs=pl.BlockSpec((CHUNK, width), lambda b, c: (row(b, c), 0)),
            out_shape=jax.ShapeDtypeStruct((n, width), out_dtype),
            scratch_shapes=[pltpu.VMEM((4, RET_HEADS, CHUNK, CHUNK), F32),
                            pltpu.VMEM((RET_HEADS, RET_DIM, RET_DIM), F32)],
            compiler_params=_params("arbitrary", "arbitrary"),
            name="ret_bwd" if rev else "ret_fwd",
        )(log_gamma, proj, proj, proj, cos, sin, *extra_in)

    y_f = call(False, (), (), F32)
    return call(True, (proj, y_f), (COL_RG // 4, 0), BF16)


def _head_mean_square(x, ones_bd):
    x2 = x * x
    hi = x2.astype(BF16)
    lo = (x2 - hi.astype(F32)).astype(BF16)
    tot = jnp.dot(hi, ones_bd, preferred_element_type=F32) + jnp.dot(lo, ones_bd, preferred_element_type=F32)
    return tot * (1.0 / ATT_HEAD_DIM)


def _pad_heads(x, n_heads, fill=None):
    pad = jnp.zeros((x.shape[0], LANES - ATT_HEAD_DIM), x.dtype) if fill is None else fill
    parts = []
    for h in range(n_heads):
        parts += [x[:, h * ATT_HEAD_DIM:(h + 1) * ATT_HEAD_DIM], pad]
    return jnp.concatenate(parts, axis=1)


def _att_prep_kernel(q_ref, k_ref, v_ref, qn_ref, kn_ref, cos_ref, sin_ref, ones_ref, qo_ref, ko_ref, vo_ref):
    rep = ATT_HEADS // ATT_KV_HEADS
    cos1 = cos_ref[...]
    sin1 = sin_ref[...]
    q = q_ref[...].astype(F32)
    q = q * lax.rsqrt(_head_mean_square(q, ones_ref[...]) + EPS) * qn_ref[...]
    q = _rotate(q, jnp.concatenate([cos1] * rep, axis=1), jnp.concatenate([sin1] * rep, axis=1), ATT_HEAD_DIM // 4)
    qo_ref[...] = _pad_heads((q * (ATT_HEAD_DIM ** -0.5)).astype(qo_ref.dtype), ATT_HEADS)
    k = k_ref[...].astype(F32)
    kw = ATT_KV_HEADS * ATT_HEAD_DIM
    k = k * lax.rsqrt(_head_mean_square(k, ones_ref[0:kw, 0:kw]) + EPS) * kn_ref[...]
    k = _rotate(k, cos1, sin1, ATT_HEAD_DIM // 4)
    ko_ref[...] = _pad_heads(k.astype(ko_ref.dtype), ATT_KV_HEADS)
    v = v_ref[...]
    vo_ref[...] = _pad_heads(v, ATT_KV_HEADS, fill=jnp.ones((v.shape[0], LANES - ATT_HEAD_DIM), v.dtype))


def _flash_kernel(tq, tk, q_ref, k_ref, v_ref, o_ref, m_scr, acc_scr):
    ki = pl.program_id(2)
    rep = ATT_HEADS // ATT_KV_HEADS

    @pl.when(ki == 0)
    def _():
        m_scr[...] = jnp.full_like(m_scr, -jnp.inf)
        acc_scr[...] = jnp.zeros_like(acc_scr)

    for g in range(ATT_KV_HEADS):
        qs = jnp.concatenate([q_ref[:, (rep * g + j) * LANES:(rep * g + j + 1) * LANES] for j in range(rep)], axis=0)
        kg = k_ref[:, g * LANES:(g + 1) * LANES]
        vg = v_ref[:, g * LANES:(g + 1) * LANES]
        s = lax.dot_general(qs, kg, (((1,), (1,)), ((), ())), preferred_element_type=F32)
        m_prev = m_scr[g]
        m_new = jnp.maximum(m_prev, jnp.max(s, axis=1, keepdims=True))
        p = jnp.exp(s - jnp.tile(m_new, (1, tk // LANES)))
        alpha = jnp.exp(m_prev - m_new)
        acc_scr[g] = alpha * acc_scr[g] + jnp.dot(p.astype(BF16), vg, preferred_element_type=F32)
        m_scr[g] = m_new

    @pl.when(ki == pl.num_programs(2) - 1)
    def _():
        outs = []
        for g in range(ATT_KV_HEADS):
            for j in range(rep):
                rows = acc_scr[g, j * tq:(j + 1) * tq, :]
                outs.append(rows[:, :ATT_HEAD_DIM] / rows[:, ATT_HEAD_DIM:])
        o_ref[...] = jnp.concatenate(outs, axis=1).astype(o_ref.dtype)


def _att_mixer(proj, bsz, seq, q_norm, k_norm, cos, sin, ones_bd):
    n = bsz * seq
    qw = ATT_HEADS * ATT_HEAD_DIM
    kw = ATT_KV_HEADS * ATT_HEAD_DIM
    qpw = ATT_HEADS * LANES
    kpw = ATT_KV_HEADS * LANES
    tm = min(512, seq)
    nt = seq // tm
    qn_row = jnp.tile(q_norm, ATT_HEADS)[None, :]
    kn_row = jnp.tile(k_norm, ATT_KV_HEADS)[None, :]
    q_pad, k_pad, v_ext = pl.pallas_call(
        _att_prep_kernel,
        grid=(bsz, nt),
        in_specs=[pl.BlockSpec((tm, qw), lambda b, i: (b * nt + i, COL_AQ // 4)),
                  pl.BlockSpec((tm, kw), lambda b, i: (b * nt + i, COL_AK)),
                  pl.BlockSpec((tm, kw), lambda b, i: (b * nt + i, COL_AV)),
                  pl.BlockSpec((1, qw), lambda b, i: (0, 0)),
                  pl.BlockSpec((1, kw), lambda b, i: (0, 0)),
                  pl.BlockSpec((tm, kw), lambda b, i: (i, 0)),
                  pl.BlockSpec((tm, kw), lambda b, i: (i, 0)),
                  pl.BlockSpec((qw, qw), lambda b, i: (0, 0))],
        out_specs=[pl.BlockSpec((tm, qpw), lambda b, i: (b * nt + i, 0)),
                   pl.BlockSpec((tm, kpw), lambda b, i: (b * nt + i, 0)),
                   pl.BlockSpec((tm, kpw), lambda b, i: (b * nt + i, 0))],
        out_shape=[jax.ShapeDtypeStruct((n, qpw), BF16), jax.ShapeDtypeStruct((n, kpw), BF16),
                   jax.ShapeDtypeStruct((n, kpw), BF16)],
        compiler_params=_params("arbitrary", "arbitrary"),
        name="att_prep",
    )(proj, proj, proj, qn_row, kn_row, cos, sin, ones_bd)

    tq = min(512, seq)
    tk = min(2048, seq)
    nq = seq // tq
    nk = seq // tk
    rep = ATT_HEADS // ATT_KV_HEADS
    return pl.pallas_call(
        functools.partial(_flash_kernel, tq, tk),
        grid=(bsz, nq, nk),
        in_specs=[pl.BlockSpec((tq, qpw), lambda b, i, j: (b * nq + i, 0)),
                  pl.BlockSpec((tk, kpw), lambda b, i, j: (b * nk + j, 0)),
                  pl.BlockSpec((tk, kpw), lambda b, i, j: (b * nk + j, 0))],
        out_specs=pl.BlockSpec((tq, qw), lambda b, i, j: (b * nq + i, 0)),
        out_shape=jax.ShapeDtypeStruct((n, qw), BF16),
        scratch_shapes=[pltpu.VMEM((ATT_KV_HEADS, rep * tq, LANES), F32),
                        pltpu.VMEM((ATT_KV_HEADS, rep * tq, LANES), F32)],
        compiler_params=_params("arbitrary", "arbitrary", "arbitrary"),
        name="flash_attention",
    )(q_pad, k_pad, v_ext)


def _fft_split(seq):
    l1 = 1 << (int(math.log2(seq)) // 2)
    return l1, seq // l1


def _fft_tables(seq):
    l1, l2 = _fft_split(seq)
    c = np.arange(FNET_GROUP_DIM)
    ang = 2.0 * np.pi * ((c[:, None] * c[None, :]) % FNET_GROUP_DIM) / FNET_GROUP_DIM
    chan = np.concatenate([np.cos(ang), -np.sin(ang)], axis=1)
    k1 = np.arange(l1)
    ang1 = 2.0 * np.pi * ((k1[:, None] * k1[None, :]) % l1) / l1
    t2 = np.arange(l2)
    k2 = np.arange(l2)
    kk = k1[:, None, None] + l1 * k2[None, :, None]
    ang2 = 2.0 * np.pi * ((kk * t2[None, None, :]) % seq) / seq
    stage2 = np.concatenate([np.cos(ang2), np.sin(ang2)], axis=2)
    to = lambda a: jnp.asarray(a, dtype=BF16)
    return to(chan), to(np.cos(ang1)), to(np.sin(ang1)), to(stage2)


FFT_ROWS = 8


def _fft_chan_kernel(f_ref, chan_ref, zr_ref, zi_ref):
    zr, zi = [], []
    for g in range(FNET_GROUPS):
        z = jnp.dot(f_ref[:, g * FNET_GROUP_DIM:(g + 1) * FNET_GROUP_DIM], chan_ref[...],
                    preferred_element_type=F32)
        zr.append(z[:, :FNET_GROUP_DIM])
        zi.append(z[:, FNET_GROUP_DIM:])
    zr_ref[...] = jnp.concatenate(zr, axis=1)
    zi_ref[...] = jnp.concatenate(zi, axis=1)


def _fft_a_kernel(zr_ref, zi_ref, c1_ref, s1_ref, ur_ref, ui_ref):
    c1 = c1_ref[...]
    s1 = s1_ref[...]
    dot = functools.partial(jnp.dot, preferred_element_type=F32)
    for j in range(FFT_ROWS):
        zr = zr_ref[:, j, :].astype(BF16)
        zi = zi_ref[:, j, :].astype(BF16)
        ur_ref[:, j, :] = dot(c1, zr) + dot(s1, zi)
        ui_ref[:, j, :] = dot(c1, zi) - dot(s1, zr)


def _fft_b_kernel(scale, ur_ref, ui_ref, m_ref, o_ref):
    for j in range(FFT_ROWS):
        u = jnp.concatenate([ur_ref[j], ui_ref[j]], axis=0).astype(BF16)
        o_ref[:, j, :] = jnp.dot(m_ref[j], u, preferred_element_type=F32) * scale


def _fft_mixer(proj, bsz, seq, tables):
    chan, c1, s1, stage2 = tables
    l1, l2 = _fft_split(seq)
    n = bsz * seq
    w = FNET_GROUPS * FNET_GROUP_DIM
    tm = min(1024, n)
    z_r, z_i = pl.pallas_call(
        _fft_chan_kernel,
        grid=(n // tm,),
        in_specs=[pl.BlockSpec((tm, w), lambda i: (i, COL_F // 4)),
                  pl.BlockSpec((FNET_GROUP_DIM, 2 * FNET_GROUP_DIM), lambda i: (0, 0))],
        out_specs=[pl.BlockSpec((tm, w), lambda i: (i, 0))] * 2,
        out_shape=[jax.ShapeDtypeStruct((n, w), F32)] * 2,
        compiler_params=_params("arbitrary"),
        name="fft_channels",
    )(proj, chan)
    slab = pl.BlockSpec((None, l1, FFT_ROWS, w), lambda b, t: (b, 0, t, 0))
    u_r, u_i = pl.pallas_call(
        _fft_a_kernel,
        grid=(bsz, l2 // FFT_ROWS),
        in_specs=[slab, slab, pl.BlockSpec((l1, l1), lambda b, t: (0, 0)), pl.BlockSpec((l1, l1), lambda b, t: (0, 0))],
        out_specs=[slab, slab],
        out_shape=[jax.ShapeDtypeStruct((bsz, l1, l2, w), F32)] * 2,
        compiler_params=_params("arbitrary", "arbitrary"),
        name="fft_stage_a",
    )(z_r.reshape(bsz, l1, l2, w), z_i.reshape(bsz, l1, l2, w), c1, s1)
    scale = 1.0 / math.sqrt(seq * FNET_GROUP_DIM)
    rows = pl.BlockSpec((None, FFT_ROWS, l2, w), lambda b, i: (b, i, 0, 0))
    y = pl.pallas_call(
        functools.partial(_fft_b_kernel, scale),
        grid=(bsz, l1 // FFT_ROWS),
        in_specs=[rows, rows, pl.BlockSpec((FFT_ROWS, l2, 2 * l2), lambda b, i: (i, 0, 0))],
        out_specs=pl.BlockSpec((None, l2, FFT_ROWS, w), lambda b, i: (b, 0, i, 0)),
        out_shape=jax.ShapeDtypeStruct((bsz, l2, l1, w), F32),
        compiler_params=_params("arbitrary", "arbitrary"),
        name="fft_stage_b",
    )(u_r, u_i, stage2)
    return y.reshape(n, w)


def _merge_kernel(x_ref, g0_ref, g1_ref, g2_ref, g3_ref, o0_ref, o1_ref, o2_ref, o3_ref,
                  wb_ref, wo_ref, nf_ref, wr_ref, xo_ref, u_ref, aff_ref):
    merged = None
    for kbr, (g_ref, o_ref) in enumerate(((g0_ref, o0_ref), (g1_ref, o1_ref), (g2_ref, o2_ref), (g3_ref, o3_ref))):
        term = jax.nn.sigmoid(g_ref[...].astype(F32)) * jnp.dot(o_ref[...].astype(BF16), wb_ref[kbr], preferred_element_type=F32)
        merged = term if merged is None else merged + term
    x = x_ref[...] + jnp.dot(merged.astype(BF16), wo_ref[...], preferred_element_type=F32)
    xo_ref[...] = x
    ms = jnp.mean(x * x, axis=-1, keepdims=True)
    u = (x * lax.rsqrt(ms + EPS) * nf_ref[...]).astype(BF16)
    u_ref[...] = u
    logits = jnp.dot(u, wr_ref[...], preferred_element_type=F32)
    lane = lax.broadcasted_iota(jnp.int32, logits.shape, 1)
    logits = jnp.where(lane < N_EXPERTS, logits, -jnp.inf)
    e = jnp.exp(logits - jnp.max(logits, axis=-1, keepdims=True))
    aff_ref[...] = e / jnp.sum(e, axis=-1, keepdims=True)


def _merge(x2, proj, branches, w_branch, w_out, norm_ffn, w_router, layer):
    n, d = x2.shape
    tm = min(256, n)
    bw = SSD_WIDTH
    gate_spec = lambda kbr: pl.BlockSpec((tm, d), lambda i: (i, COL_GATES // 8 + kbr))
    br_spec = pl.BlockSpec((tm, bw), lambda i: (i, 0))
    return pl.pallas_call(
        _merge_kernel,
        grid=(n // tm,),
        in_specs=[pl.BlockSpec((tm, d), lambda i: (i, 0))] + [gate_spec(k) for k in range(N_BRANCHES)]
                 + [br_spec] * N_BRANCHES
                 + [pl.BlockSpec((None, N_BRANCHES, bw, d), lambda i: (layer, 0, 0, 0)),
                    pl.BlockSpec((None, d, d), lambda i: (layer, 0, 0)),
                    pl.BlockSpec((None, 1, d), lambda i: (layer, 0, 0)),
                    pl.BlockSpec((None, d, LANES), lambda i: (layer, 0, 0))],
        out_specs=[pl.BlockSpec((tm, d), lambda i: (i, 0)), pl.BlockSpec((tm, d), lambda i: (i, 0)),
                   pl.BlockSpec((tm, LANES), lambda i: (i, 0))],
        out_shape=[jax.ShapeDtypeStruct((n, d), F32), jax.ShapeDtypeStruct((n, d), BF16),
                   jax.ShapeDtypeStruct((n, LANES), F32)],
        compiler_params=_params("arbitrary"),
        name="merge_out",
    )(x2, proj, proj, proj, proj, *branches, w_branch, w_out, norm_ffn, w_router)


def _ffn_kernel(xe_ref, gate_ref, wu_ref, wd_ref, o_ref):
    h = jnp.dot(xe_ref[...], wu_ref[...], preferred_element_type=F32)
    act = (_silu(h[:, :EXPERT_FF]) * h[:, EXPERT_FF:]).astype(BF16)
    o_ref[...] = jnp.dot(act, wd_ref[...], preferred_element_type=F32) * gate_ref[...]


def _expert_ffn(xe, gate, w_up, w_down, layer):
    e, cap, d = xe.shape
    tc = min(512, cap)
    return pl.pallas_call(
        _ffn_kernel,
        grid=(e, cap // tc),
        in_specs=[pl.BlockSpec((None, tc, d), lambda i, j: (i, j, 0)),
                  pl.BlockSpec((None, tc, 1), lambda i, j: (i, j, 0)),
                  pl.BlockSpec((None, None, d, 2 * EXPERT_FF), lambda i, j: (layer, i, 0, 0)),
                  pl.BlockSpec((None, None, EXPERT_FF, d), lambda i, j: (layer, i, 0, 0))],
        out_specs=pl.BlockSpec((None, tc, d), lambda i, j: (i, j, 0)),
        out_shape=jax.ShapeDtypeStruct((e, cap, d), F32),
        compiler_params=_params("arbitrary", "arbitrary"),
        name="expert_ffn",
    )(xe, gate, w_up, w_down)


COMBINE_TILE = 128
COMBINE_UNROLL = 4


def _combine_window_start(first_row, total_rows):
    return pl.multiple_of(jnp.minimum((first_row // COMBINE_TILE) * COMBINE_TILE, total_rows - 2 * COMBINE_TILE),
                          COMBINE_TILE)


def _combine_kernel(total_rows, tok_ref, r0_ref, x_ref, *refs):
    ye_refs, o_ref = refs[:N_EXPERTS], refs[N_EXPERTS]
    t = pl.program_id(0)
    tok0 = t * COMBINE_TILE
    o_ref[...] = x_ref[...]

    for e in range(N_EXPERTS):
        lo = r0_ref[t * N_EXPERTS + e]
        hi = r0_ref[(t + 1) * N_EXPERTS + e]
        buf_ref = ye_refs[e]
        row0 = _combine_window_start(lo, total_rows)

        def body_n(i, carry):
            r = lo + COMBINE_UNROLL * i
            toks = [tok_ref[r + j] - tok0 for j in range(COMBINE_UNROLL)]
            vals = [o_ref[pl.ds(toks[j], 1), :] + buf_ref[pl.ds(r + j - row0, 1), :] for j in range(COMBINE_UNROLL)]
            for j in range(COMBINE_UNROLL):
                o_ref[pl.ds(toks[j], 1), :] = vals[j]
            return carry

        n_full = (hi - lo) // COMBINE_UNROLL
        lax.fori_loop(0, n_full, body_n, 0)

        def body_1(r, carry):
            tk = tok_ref[r] - tok0
            o_ref[pl.ds(tk, 1), :] = o_ref[pl.ds(tk, 1), :] + buf_ref[pl.ds(r - row0, 1), :]
            return carry

        lax.fori_loop(lo + COMBINE_UNROLL * n_full, hi, body_1, 0)


def _combine(x2, ye, idx):
    n, d = x2.shape
    n_exp, cap = idx.shape
    nt = n // COMBINE_TILE
    total_rows = n_exp * cap
    bounds = jnp.arange(nt + 1, dtype=jnp.int32) * COMBINE_TILE
    r0 = jnp.sum((idx[:, :, None] < bounds[None, None, :]).astype(jnp.int32), axis=1)
    r0 = (r0 + (jnp.arange(n_exp, dtype=jnp.int32) * cap)[:, None]).T.reshape(-1)

    def ye_spec(e):
        return pl.BlockSpec((pl.Element(2 * COMBINE_TILE), pl.Element(d)),
                            lambda t, tok, r0s: (_combine_window_start(r0s[t * N_EXPERTS + e], total_rows), 0))

    grid_spec = pltpu.PrefetchScalarGridSpec(
        num_scalar_prefetch=2,
        grid=(nt,),
        in_specs=[pl.BlockSpec((COMBINE_TILE, d), lambda t, tok, r0s: (t, 0))] + [ye_spec(e) for e in range(n_exp)],
        out_specs=pl.BlockSpec((COMBINE_TILE, d), lambda t, tok, r0s: (t, 0)),
    )
    return pl.pallas_call(
        functools.partial(_combine_kernel, total_rows),
        grid_spec=grid_spec,
        out_shape=jax.ShapeDtypeStruct((n, d), x2.dtype),
        input_output_aliases={2: 0},
        compiler_params=_params("arbitrary"),
        name="expert_combine",
    )(idx.reshape(-1).astype(jnp.int32), r0, x2, *([ye] * n_exp))


def _norm_kernel(x_ref, w_ref, o_ref):
    x = x_ref[...]
    ms = jnp.mean(x * x, axis=-1, keepdims=True)
    o_ref[...] = x * lax.rsqrt(ms + EPS) * w_ref[...]


def _final_norm(x2, w):
    n, d = x2.shape
    tm = min(1024, n)
    return pl.pallas_call(
        _norm_kernel,
        grid=(n // tm,),
        in_specs=[pl.BlockSpec((tm, d), lambda i: (i, 0)), pl.BlockSpec((1, d), lambda i: (0, 0))],
        out_specs=pl.BlockSpec((tm, d), lambda i: (i, 0)),
        out_shape=jax.ShapeDtypeStruct((n, d), F32),
        compiler_params=_params("arbitrary"),
        name="final_norm",
    )(x2, w)


def _pad_lanes(v):
    return jnp.pad(v, (0, LANES - v.shape[0]))[None, :]


def _prepare(norm_mix, w_in, conv_w, conv_b, ssd_dt_bias, ssd_a_log, ssd_d, ssd_norm, ret_log2_decay,
             attn_q_norm, attn_k_norm, w_branch, w_out, norm_ffn, w_router, w_up, w_down, norm_final):
    p = dict(
        norm_mix=norm_mix[:, None, :],
        w_proj=_repack_proj_weights(w_in),
        conv_w=conv_w, conv_b=conv_b[:, None, :],
        dtb=[_pad_lanes(ssd_dt_bias[l].reshape(-1)) for l in range(DEPTH)],
        alog=[_pad_lanes(ssd_a_log[l].reshape(-1)) for l in range(DEPTH)],
        dsk=[jnp.repeat(ssd_d[l], SSD_HEAD_DIM)[None, :] for l in range(DEPTH)],
        ssd_norm=ssd_norm[:, None, :],
        log_gamma=jnp.log1p(-jnp.exp2(ret_log2_decay.astype(F32))).reshape(DEPTH, 2 * RET_HEADS),
        q_norm=attn_q_norm, k_norm=attn_k_norm,
        w_branch=w_branch.astype(BF16), w_out=w_out.astype(BF16),
        norm_ffn=norm_ffn[:, None, :],
        w_router=jnp.pad(w_router, ((0, 0), (0, 0), (0, LANES - N_EXPERTS))).astype(BF16),
        w_up=w_up.astype(BF16), w_down=w_down.astype(BF16),
        norm_final=norm_final[None, :],
    )
    qw = ATT_HEADS * ATT_HEAD_DIM
    hid = np.arange(qw) // ATT_HEAD_DIM
    p["ones_bd"] = jnp.asarray(hid[:, None] == hid[None, :], dtype=BF16)
    return p


def _trunk(x, p):
    bsz, seq, d = x.shape
    n = bsz * seq
    cap = (EC_CAPACITY_FACTOR * n) // N_EXPERTS
    ret_cos, ret_sin = _rotary_tables(seq, RET_DIM)
    att_cos, att_sin = [jnp.tile(t, (1, ATT_KV_HEADS)) for t in _rotary_tables(seq, ATT_HEAD_DIM)]
    fft_tables = _fft_tables(seq)
    x2 = x.reshape(n, d)
    for l in range(DEPTH):
        proj = _norm_proj(x2, p["norm_mix"], p["w_proj"], l)
        o_ssd = _ssd_mixer(proj, bsz, seq, p["conv_w"][l], p["conv_b"][l], p["dtb"][l], p["alog"][l], p["dsk"][l],
                           p["ssd_norm"][l])
        o_ret = _ret_mixer(proj, bsz, seq, p["log_gamma"][l], ret_cos, ret_sin)
        o_att = _att_mixer(proj, bsz, seq, p["q_norm"][l], p["k_norm"][l], att_cos, att_sin, p["ones_bd"])
        o_fft = _fft_mixer(proj, bsz, seq, fft_tables)
        x2, u, aff = _merge(x2, proj, (o_ssd, o_ret, o_att, o_fft), p["w_branch"], p["w_out"], p["norm_ffn"],
                            p["w_router"], l)
        gate, idx = lax.top_k(aff[:, :N_EXPERTS].T, cap)
        idx, gate = lax.sort_key_val(idx, gate, dimension=1)
        xe = jnp.take(u, idx.reshape(-1), axis=0).reshape(N_EXPERTS, cap, d)
        ye = _expert_ffn(xe, gate[..., None], p["w_up"], p["w_down"], l)
        x2 = _combine(x2, ye.reshape(-1, d), idx)
    return _final_norm(x2, p["norm_final"]).reshape(bsz, seq, d)


def kernel(x_prompt, x_sample, norm_mix, w_in, conv_w, conv_b, ssd_dt_bias, ssd_a_log, ssd_d, ssd_norm, ret_log2_decay, attn_q_norm, attn_k_norm, w_branch, w_out, norm_ffn, w_router, w_up, w_down, norm_final):
    p = _prepare(norm_mix, w_in, conv_w, conv_b, ssd_dt_bias, ssd_a_log, ssd_d, ssd_norm, ret_log2_decay,
                 attn_q_norm, attn_k_norm, w_branch, w_out, norm_ffn, w_router, w_up, w_down, norm_final)
    return (_trunk(x_prompt, p), _trunk(x_sample, p))
```

```python
import functools
import math

import numpy as np
import jax
import jax.numpy as jnp
from jax import lax
from jax.experimental import pallas as pl
from jax.experimental.pallas import tpu as pltpu

F32 = jnp.float32
BF16 = jnp.bfloat16

D_MODEL = 1024
DEPTH = 4
GRID_W = 64
CHUNK = 128
EPS = 1e-6
ROPE_THETA = 10000.0

SSD_HEADS = 8
SSD_HEAD_DIM = 64
SSD_WIDTH = 512
SSD_GROUPS = 2
SSD_STATE = 128
SSD_CONV = 5
SSD_XBC = 1024

RET_HEADS = 4
RET_DIM = 128

ATT_HEADS = 8
ATT_KV_HEADS = 2
ATT_HEAD_DIM = 64

FNET_GROUPS = 4
FNET_GROUP_DIM = 128

N_BRANCHES = 4
N_EXPERTS = 16
EXPERT_FF = 1024
EC_CAPACITY_FACTOR = 2

IN_SIZES = (512, 1024, 8, 8, 512, 512, 512, 512, 512, 128, 128, 512, 4096)
D_IN = sum(IN_SIZES)

LANES = 128
CONV_HALO = 16

COL_XBC, COL_Z, COL_RQ, COL_RK, COL_RV, COL_RG, COL_AQ = 0, 8, 12, 16, 20, 24, 28
COL_GATES, COL_F, COL_AK, COL_AV, COL_DT = 32, 64, 68, 69, 70
PROJ_COLS = 72 * LANES


def _repack_proj_weights(w_in):
    offs = np.concatenate([[0], np.cumsum(IN_SIZES)])
    (z, xbc, dtf, dtb, rq, rk, rv, rg, aq, ak, av, f, gates) = [w_in[..., offs[i]:offs[i + 1]] for i in range(13)]
    pad = jnp.zeros(w_in.shape[:-1] + (2 * LANES - 16,), w_in.dtype)
    w = jnp.concatenate([xbc, z, rq, rk, rv, rg, aq, gates, f, ak, av, dtf, dtb, pad], axis=-1)
    assert w.shape[-1] == PROJ_COLS
    return w.astype(BF16)


def _params(*sem):
    return pltpu.CompilerParams(dimension_semantics=sem, vmem_limit_bytes=56 * 1024 * 1024)


def _silu(x):
    return x * jax.nn.sigmoid(x)


def _proj_kernel(x_ref, nw_ref, w_ref, o_ref, u_scr):
    @pl.when(pl.program_id(1) == 0)
    def _():
        x = x_ref[...]
        ms = jnp.mean(x * x, axis=-1, keepdims=True)
        u_scr[...] = (x * lax.rsqrt(ms + EPS) * nw_ref[...]).astype(BF16)

    o_ref[...] = jnp.dot(u_scr[...], w_ref[...], preferred_element_type=F32).astype(o_ref.dtype)


def _norm_proj(x2, nw, w_all, layer):
    n, d = x2.shape
    cols = w_all.shape[-1]
    tm = min(1024, n)
    tn = 1536
    return pl.pallas_call(
        _proj_kernel,
        grid=(n // tm, cols // tn),
        in_specs=[pl.BlockSpec((tm, d), lambda i, j: (i, 0)),
                  pl.BlockSpec((None, 1, d), lambda i, j: (layer, 0, 0)),
                  pl.BlockSpec((None, d, tn), lambda i, j: (layer, 0, j))],
        out_specs=pl.BlockSpec((tm, tn), lambda i, j: (i, j)),
        out_shape=jax.ShapeDtypeStruct((n, cols), BF16),
        scratch_shapes=[pltpu.VMEM((tm, d), BF16)],
        compiler_params=_params("arbitrary", "arbitrary"),
        name="norm_proj",
    )(x2, nw, w_all)


def _cumsum_rows(x, reverse):
    n = x.shape[0]
    row = lax.broadcasted_iota(jnp.int32, x.shape, 0)
    d = 1
    while d < n:
        if reverse:
            x = x + jnp.where(row < n - d, pltpu.roll(x, n - d, axis=0), 0.0)
        else:
            x = x + jnp.where(row >= d, pltpu.roll(x, d, axis=0), 0.0)
        d *= 2
    return x


def _softplus(x):
    return jnp.maximum(x, 0.0) + jnp.log1p(jnp.exp(-jnp.abs(x)))


def _ssd_kernel(rev, *refs):
    if rev:
        (act_ref, dt_ref, dtb_ref, alog_ref, z_ref, yf_ref, nw_ref, o_ref, h_scr) = refs
    else:
        (prev_ref, main_ref, next_ref, dt_ref, cw_ref, cb_ref, dtb_ref, alog_ref, dsk_ref,
         o_ref, act_ref, ext_scr, h_scr) = refs
    c = pl.program_id(1)
    nc = pl.num_programs(1)
    cc = nc - 1 - c if rev else c

    @pl.when(c == 0)
    def _():
        h_scr[...] = jnp.zeros_like(h_scr)

    if rev:
        act = act_ref[...].astype(F32)
    else:
        ext_scr[0:CONV_HALO, :] = jnp.where(cc == 0, 0.0, prev_ref[...].astype(F32))
        ext_scr[CONV_HALO:CONV_HALO + CHUNK, :] = main_ref[...].astype(F32)
        ext_scr[CONV_HALO + CHUNK:, :] = jnp.where(cc == nc - 1, 0.0, next_ref[...].astype(F32))
        acc = jnp.broadcast_to(cb_ref[...], (CHUNK, SSD_XBC))
        for k in range(SSD_CONV):
            acc = acc + cw_ref[k:k + 1, :] * ext_scr[pl.ds(CONV_HALO - SSD_CONV // 2 + k, CHUNK), :]
        act = _silu(acc)
        act_ref[...] = act.astype(act_ref.dtype)
    gn = SSD_GROUPS * SSD_STATE
    x = act[:, :SSD_WIDTH]
    b_mat = act[:, SSD_WIDTH:SSD_WIDTH + gn]
    c_mat = act[:, SSD_WIDTH + gn:]

    off = SSD_HEADS if rev else 0
    dt = _softplus(dt_ref[...].astype(F32) + dtb_ref[...])
    a = -jnp.exp(alog_ref[...])
    cs = _cumsum_rows(dt * a, rev)
    cs_t = cs.T
    rowi = lax.broadcasted_iota(jnp.int32, (CHUNK, CHUNK), 0)
    coli = lax.broadcasted_iota(jnp.int32, (CHUNK, CHUNK), 1)
    mask = (rowi < coli) if rev else (rowi >= coli)

    def expand(v):
        return jnp.concatenate(
            [jnp.broadcast_to(v[:, off + h:off + h + 1], (CHUNK, SSD_HEAD_DIM)) for h in range(SSD_HEADS)], axis=1)

    dt_e = expand(dt)
    cs_e = expand(cs)
    edge = cs_e[0:1, :] if rev else cs_e[CHUNK - 1:CHUNK, :]
    xdt = x * dt_e
    xst = (xdt * jnp.exp(edge - cs_e)).astype(BF16)
    xdt = xdt.astype(BF16)
    e_in = jnp.exp(cs_e)
    e_chunk = jnp.exp(edge)

    r = SSD_HEADS // SSD_GROUPS
    gw = r * SSD_HEAD_DIM
    ys = []
    for g in range(SSD_GROUPS):
        bg = b_mat[:, g * SSD_STATE:(g + 1) * SSD_STATE].astype(BF16)
        cg = c_mat[:, g * SSD_STATE:(g + 1) * SSD_STATE].astype(BF16)
        scores = lax.dot_general(cg, bg, (((1,), (1,)), ((), ())), preferred_element_type=F32)
        h_prev = h_scr[g]
        y_off = jnp.dot(cg, h_prev.astype(BF16), preferred_element_type=F32) * e_in[:, g * gw:(g + 1) * gw]
        st = lax.dot_general(bg, xst[:, g * gw:(g + 1) * gw], (((0,), (0,)), ((), ())),
                             preferred_element_type=F32)
        h_scr[g] = h_prev * e_chunk[:, g * gw:(g + 1) * gw] + st
        for j in range(r):
            h = g * r + j
            colb = jnp.broadcast_to(cs[:, off + h:off + h + 1], (CHUNK, CHUNK))
            rowb = cs_t[off + h:off + h + 1, :]
            decay = jnp.exp(jnp.where(mask, colb - rowb, -jnp.inf))
            m = (scores * decay).astype(BF16)
            y_d = jnp.dot(m, xdt[:, h * SSD_HEAD_DIM:(h + 1) * SSD_HEAD_DIM], preferred_element_type=F32)
            ys.append(y_d + y_off[:, j * SSD_HEAD_DIM:(j + 1) * SSD_HEAD_DIM])
    y = jnp.concatenate(ys, axis=1)

    if rev:
        y = (y + yf_ref[...]) * _silu(z_ref[...].astype(F32))
        ms = jnp.mean(y * y, axis=-1, keepdims=True)
        o_ref[...] = (y * lax.rsqrt(ms + EPS) * nw_ref[...]).astype(o_ref.dtype)
    else:
        o_ref[...] = y + x * dsk_ref[...]


def _ssd_mixer(proj, bsz, seq, conv_w, conv_b, dtb_row, alog_row, dsk_row, norm_w):
    n = bsz * seq
    nc = seq // CHUNK
    hb = CHUNK // CONV_HALO
    nhalo = n // CONV_HALO

    state = pltpu.VMEM((SSD_GROUPS, SSD_STATE, SSD_WIDTH // SSD_GROUPS), F32)
    const = lambda shape: pl.BlockSpec(shape, lambda b, c: (0, 0))
    row = lambda b, c: b * nc + c
    y_f, act = pl.pallas_call(
        functools.partial(_ssd_kernel, False),
        grid=(bsz, nc),
        in_specs=[
            pl.BlockSpec((CONV_HALO, SSD_XBC), lambda b, c: (jnp.maximum(row(b, c) * hb - 1, 0), COL_XBC // 8)),
            pl.BlockSpec((CHUNK, SSD_XBC), lambda b, c: (row(b, c), COL_XBC // 8)),
            pl.BlockSpec((CONV_HALO, SSD_XBC), lambda b, c: (jnp.minimum((row(b, c) + 1) * hb, nhalo - 1), COL_XBC // 8)),
            pl.BlockSpec((CHUNK, LANES), lambda b, c: (row(b, c), COL_DT)),
            const((SSD_CONV, SSD_XBC)), const((1, SSD_XBC)), const((1, LANES)), const((1, LANES)), const((1, SSD_WIDTH)),
        ],
        out_specs=[pl.BlockSpec((CHUNK, SSD_WIDTH), lambda b, c: (row(b, c), 0)),
                   pl.BlockSpec((CHUNK, SSD_XBC), lambda b, c: (row(b, c), 0))],
        out_shape=[jax.ShapeDtypeStruct((n, SSD_WIDTH), F32), jax.ShapeDtypeStruct((n, SSD_XBC), BF16)],
        scratch_shapes=[pltpu.VMEM((CHUNK + 2 * CONV_HALO, SSD_XBC), F32), state],
        compiler_params=_params("arbitrary", "arbitrary"),
        name="ssd_fwd",
    )(proj, proj, proj, proj, conv_w, conv_b, dtb_row, alog_row, dsk_row)

    rrow = lambda b, c: b * nc + (nc - 1 - c)
    return pl.pallas_call(
        functools.partial(_ssd_kernel, True),
        grid=(bsz, nc),
        in_specs=[
            pl.BlockSpec((CHUNK, SSD_XBC), lambda b, c: (rrow(b, c), 0)),
            pl.BlockSpec((CHUNK, LANES), lambda b, c: (rrow(b, c), COL_DT)),
            const((1, LANES)), const((1, LANES)),
            pl.BlockSpec((CHUNK, SSD_WIDTH), lambda b, c: (rrow(b, c), COL_Z // 4)),
            pl.BlockSpec((CHUNK, SSD_WIDTH), lambda b, c: (rrow(b, c), 0)),
            const((1, SSD_WIDTH)),
        ],
        out_specs=pl.BlockSpec((CHUNK, SSD_WIDTH), lambda b, c: (rrow(b, c), 0)),
        out_shape=jax.ShapeDtypeStruct((n, SSD_WIDTH), BF16),
        scratch_shapes=[state],
        compiler_params=_params("arbitrary", "arbitrary"),
        name="ssd_bwd",
    )(act, proj, dtb_row, alog_row, proj, y_f, norm_w)


def _rotary_tables(seq, head_dim):
    n_freq = head_dim // 4
    freqs = ROPE_THETA ** (-jnp.arange(n_freq, dtype=F32) / n_freq)
    t = jnp.arange(seq)
    ang_row = (t // GRID_W).astype(F32)[:, None] * freqs
    ang_col = (t % GRID_W).astype(F32)[:, None] * freqs
    cos = jnp.concatenate([jnp.cos(ang_row)] * 2 + [jnp.cos(ang_col)] * 2, axis=1)
    sin = jnp.concatenate([-jnp.sin(ang_row), jnp.sin(ang_row), -jnp.sin(ang_col), jnp.sin(ang_col)], axis=1)
    return cos, sin


def _rotate(x, cos, sin_signed, quarter):
    w = x.shape[1]
    lane = lax.broadcasted_iota(jnp.int32, x.shape, 1)
    partner = jnp.where((lane % (2 * quarter)) < quarter,
                        pltpu.roll(x, w - quarter, axis=1), pltpu.roll(x, quarter, axis=1))
    return x * cos + partner * sin_signed


def _ret_kernel(rev, *refs):
    if rev:
        (lg_ref, q_ref, k_ref, v_ref, cos_ref, sin_ref, g_ref, yf_ref, o_ref, tab_scr, h_scr) = refs
    else:
        (lg_ref, q_ref, k_ref, v_ref, cos_ref, sin_ref, o_ref, tab_scr, h_scr) = refs
    first = jnp.logical_and(pl.program_id(0) == 0, pl.program_id(1) == 0)

    @pl.when(first)
    def _():
        rowi = lax.broadcasted_iota(jnp.int32, (CHUNK, CHUNK), 0)
        coli = lax.broadcasted_iota(jnp.int32, (CHUNK, CHUNK), 1)
        rowf = rowi.astype(F32)
        for h in range(RET_HEADS):
            lg = lg_ref[(RET_HEADS if rev else 0) + h]
            if rev:
                dist = jnp.where(coli > rowi, (coli - rowi).astype(F32) * lg, -jnp.inf)
                kd = rowf * lg
                qd = (CHUNK - rowf) * lg
            else:
                dist = jnp.where(rowi >= coli, (rowi - coli).astype(F32) * lg, -jnp.inf)
                kd = (CHUNK - 1 - rowf) * lg
                qd = (rowf + 1.0) * lg
            tab_scr[0, h] = jnp.exp(dist)
            tab_scr[1, h] = jnp.exp(kd)
            tab_scr[2, h] = jnp.exp(qd)
            tab_scr[3, h] = jnp.exp(jnp.full((CHUNK, CHUNK), CHUNK, F32) * lg)

    @pl.when(pl.program_id(1) == 0)
    def _():
        h_scr[...] = jnp.zeros_like(h_scr)

    cos = jnp.concatenate([cos_ref[...]] * RET_HEADS, axis=1)
    sin = jnp.concatenate([sin_ref[...]] * RET_HEADS, axis=1)
    q = _rotate(q_ref[...].astype(F32), cos, sin, RET_DIM // 4)
    k = _rotate(k_ref[...].astype(F32), cos, sin, RET_DIM // 4) * (RET_DIM ** -0.5)
    v = v_ref[...].astype(F32)
    ys = []
    for h in range(RET_HEADS):
        sl = slice(h * RET_DIM, (h + 1) * RET_DIM)
        qh = q[:, sl].astype(BF16)
        kh = k[:, sl].astype(BF16)
        vh = v[:, sl]
        s = lax.dot_general(qh, kh, (((1,), (1,)), ((), ())), preferred_element_type=F32) * tab_scr[0, h]
        inner = jnp.dot(s.astype(BF16), vh.astype(BF16), preferred_element_type=F32)
        h_prev = h_scr[h]
        cross = jnp.dot(qh, h_prev.astype(BF16), preferred_element_type=F32) * tab_scr[2, h]
        st = lax.dot_general(kh, (vh * tab_scr[1, h]).astype(BF16), (((0,), (0,)), ((), ())),
                             preferred_element_type=F32)
        h_scr[h] = h_prev * tab_scr[3, h] + st
        ys.append(inner + cross)

    if rev:
        outs = []
        for h in range(RET_HEADS):
            sl = slice(h * RET_DIM, (h + 1) * RET_DIM)
            y = ys[h] + yf_ref[:, sl]
            y = y * lax.rsqrt(jnp.mean(y * y, axis=-1, keepdims=True) + EPS)
            outs.append(y * _silu(g_ref[:, sl].astype(F32)))
        o_ref[...] = jnp.concatenate(outs, axis=1).astype(o_ref.dtype)
    else:
        o_ref[...] = jnp.concatenate(ys, axis=1)


def _ret_mixer(proj, bsz, seq, log_gamma, cos, sin):
    n = bsz * seq
    nc = seq // CHUNK
    width = RET_HEADS * RET_DIM

    def call(rev, extra_in, extra_cols, out_dtype):
        chunk = (lambda c: nc - 1 - c) if rev else (lambda c: c)
        row = lambda b, c: b * nc + chunk(c)
        pcol = lambda cb: pl.BlockSpec((CHUNK, width), lambda b, c: (row(b, c), cb))
        tab = pl.BlockSpec((CHUNK, RET_DIM), lambda b, c: (chunk(c), 0))
        in_specs = [pl.BlockSpec(memory_space=pltpu.SMEM),
                    pcol(COL_RQ // 4), pcol(COL_RK // 4), pcol(COL_RV // 4), tab, tab] + [pcol(cb) for cb in extra_cols]
        return pl.pallas_call(
            functools.partial(_ret_kernel, rev),
            grid=(bsz, nc),
            in_specs=in_specs,
            out_specs=pl.BlockSpec((CHUNK, width), lambda b, c: (row(b, c), 0)),
            out_shape=jax.ShapeDtypeStruct((n, width), out_dtype),
            scratch_shapes=[pltpu.VMEM((4, RET_HEADS, CHUNK, CHUNK), F32),
                            pltpu.VMEM((RET_HEADS, RET_DIM, RET_DIM), F32)],
            compiler_params=_params("arbitrary", "arbitrary"),
            name="ret_bwd" if rev else "ret_fwd",
        )(log_gamma, proj, proj, proj, cos, sin, *extra_in)

    y_f = call(False, (), (), F32)
    return call(True, (proj, y_f), (COL_RG // 4, 0), BF16)


def _head_mean_square(x, ones_bd):
    x2 = x * x
    hi = x2.astype(BF16)
    lo = (x2 - hi.astype(F32)).astype(BF16)
    tot = jnp.dot(hi, ones_bd, preferred_element_type=F32) + jnp.dot(lo, ones_bd, preferred_element_type=F32)
    return tot * (1.0 / ATT_HEAD_DIM)


def _pad_heads(x, n_heads, fill=None):
    pad = jnp.zeros((x.shape[0], LANES - ATT_HEAD_DIM), x.dtype) if fill is None else fill
    parts = []
    for h in range(n_heads):
        parts += [x[:, h * ATT_HEAD_DIM:(h + 1) * ATT_HEAD_DIM], pad]
    return jnp.concatenate(parts, axis=1)


def _att_prep_kernel(q_ref, k_ref, v_ref, qn_ref, kn_ref, cos_ref, sin_ref, ones_ref, qo_ref, ko_ref, vo_ref):
    rep = ATT_HEADS // ATT_KV_HEADS
    cos1 = cos_ref[...]
    sin1 = sin_ref[...]
    q = q_ref[...].astype(F32)
    q = q * lax.rsqrt(_head_mean_square(q, ones_ref[...]) + EPS) * qn_ref[...]
    q = _rotate(q, jnp.concatenate([cos1] * rep, axis=1), jnp.concatenate([sin1] * rep, axis=1), ATT_HEAD_DIM // 4)
    qo_ref[...] = _pad_heads((q * (ATT_HEAD_DIM ** -0.5)).astype(qo_ref.dtype), ATT_HEADS)
    k = k_ref[...].astype(F32)
    kw = ATT_KV_HEADS * ATT_HEAD_DIM
    k = k * lax.rsqrt(_head_mean_square(k, ones_ref[0:kw, 0:kw]) + EPS) * kn_ref[...]
    k = _rotate(k, cos1, sin1, ATT_HEAD_DIM // 4)
    ko_ref[...] = _pad_heads(k.astype(ko_ref.dtype), ATT_KV_HEADS)
    v = v_ref[...]
    vo_ref[...] = _pad_heads(v, ATT_KV_HEADS, fill=jnp.ones((v.shape[0], LANES - ATT_HEAD_DIM), v.dtype))


def _flash_kernel(tq, tk, q_ref, k_ref, v_ref, o_ref, m_scr, acc_scr):
    ki = pl.program_id(2)
    rep = ATT_HEADS // ATT_KV_HEADS

    @pl.when(ki == 0)
    def _():
        m_scr[...] = jnp.full_like(m_scr, -jnp.inf)
        acc_scr[...] = jnp.zeros_like(acc_scr)

    for g in range(ATT_KV_HEADS):
        qs = jnp.concatenate([q_ref[:, (rep * g + j) * LANES:(rep * g + j + 1) * LANES] for j in range(rep)], axis=0)
        kg = k_ref[:, g * LANES:(g + 1) * LANES]
        vg = v_ref[:, g * LANES:(g + 1) * LANES]
        s = lax.dot_general(qs, kg, (((1,), (1,)), ((), ())), preferred_element_type=F32)
        m_prev = m_scr[g]
        m_new = jnp.maximum(m_prev, jnp.max(s, axis=1, keepdims=True))
        p = jnp.exp(s - jnp.tile(m_new, (1, tk // LANES)))
        alpha = jnp.exp(m_prev - m_new)
        acc_scr[g] = alpha * acc_scr[g] + jnp.dot(p.astype(BF16), vg, preferred_element_type=F32)
        m_scr[g] = m_new

    @pl.when(ki == pl.num_programs(2) - 1)
    def _():
        outs = []
        for g in range(ATT_KV_HEADS):
            for j in range(rep):
                rows = acc_scr[g, j * tq:(j + 1) * tq, :]
                outs.append(rows[:, :ATT_HEAD_DIM] / rows[:, ATT_HEAD_DIM:])
        o_ref[...] = jnp.concatenate(outs, axis=1).astype(o_ref.dtype)


def _att_mixer(proj, bsz, seq, q_norm, k_norm, cos, sin, ones_bd):
    n = bsz * seq
    qw = ATT_HEADS * ATT_HEAD_DIM
    kw = ATT_KV_HEADS * ATT_HEAD_DIM
    qpw = ATT_HEADS * LANES
    kpw = ATT_KV_HEADS * LANES
    tm = min(512, seq)
    nt = seq // tm
    qn_row = jnp.tile(q_norm, ATT_HEADS)[None, :]
    kn_row = jnp.tile(k_norm, ATT_KV_HEADS)[None, :]
    q_pad, k_pad, v_ext = pl.pallas_call(
        _att_prep_kernel,
        grid=(bsz, nt),
        in_specs=[pl.BlockSpec((tm, qw), lambda b, i: (b * nt + i, COL_AQ // 4)),
                  pl.BlockSpec((tm, kw), lambda b, i: (b * nt + i, COL_AK)),
                  pl.BlockSpec((tm, kw), lambda b, i: (b * nt + i, COL_AV)),
                  pl.BlockSpec((1, qw), lambda b, i: (0, 0)),
                  pl.BlockSpec((1, kw), lambda b, i: (0, 0)),
                  pl.BlockSpec((tm, kw), lambda b, i: (i, 0)),
                  pl.BlockSpec((tm, kw), lambda b, i: (i, 0)),
                  pl.BlockSpec((qw, qw), lambda b, i: (0, 0))],
        out_specs=[pl.BlockSpec((tm, qpw), lambda b, i: (b * nt + i, 0)),
                   pl.BlockSpec((tm, kpw), lambda b, i: (b * nt + i, 0)),
                   pl.BlockSpec((tm, kpw), lambda b, i: (b * nt + i, 0))],
        out_shape=[jax.ShapeDtypeStruct((n, qpw), BF16), jax.ShapeDtypeStruct((n, kpw), BF16),
                   jax.ShapeDtypeStruct((n, kpw), BF16)],
        compiler_params=_params("arbitrary", "arbitrary"),
        name="att_prep",
    )(proj, proj, proj, qn_row, kn_row, cos, sin, ones_bd)

    tq = min(512, seq)
    tk = min(2048, seq)
    nq = seq // tq
    nk = seq // tk
    rep = ATT_HEADS // ATT_KV_HEADS
    return pl.pallas_call(
        functools.partial(_flash_kernel, tq, tk),
        grid=(bsz, nq, nk),
        in_specs=[pl.BlockSpec((tq, qpw), lambda b, i, j: (b * nq + i, 0)),
                  pl.BlockSpec((tk, kpw), lambda b, i, j: (b * nk + j, 0)),
                  pl.BlockSpec((tk, kpw), lambda b, i, j: (b * nk + j, 0))],
        out_specs=pl.BlockSpec((tq, qw), lambda b, i, j: (b * nq + i, 0)),
        out_shape=jax.ShapeDtypeStruct((n, qw), BF16),
        scratch_shapes=[pltpu.VMEM((ATT_KV_HEADS, rep * tq, LANES), F32),
                        pltpu.VMEM((ATT_KV_HEADS, rep * tq, LANES), F32)],
        compiler_params=_params("arbitrary", "arbitrary", "arbitrary"),
        name="flash_attention",
    )(q_pad, k_pad, v_ext)


def _fft_split(seq):
    l1 = 1 << (int(math.log2(seq)) // 2)
    return l1, seq // l1


def _fft_tables(seq):
    l1, l2 = _fft_split(seq)
    c = np.arange(FNET_GROUP_DIM)
    ang = 2.0 * np.pi * ((c[:, None] * c[None, :]) % FNET_GROUP_DIM) / FNET_GROUP_DIM
    chan = np.concatenate([np.cos(ang), -np.sin(ang)], axis=1)
    k1 = np.arange(l1)
    ang1 = 2.0 * np.pi * ((k1[:, None] * k1[None, :]) % l1) / l1
    t2 = np.arange(l2)
    k2 = np.arange(l2)
    kk = k1[:, None, None] + l1 * k2[None, :, None]
    ang2 = 2.0 * np.pi * ((kk * t2[None, None, :]) % seq) / seq
    stage2 = np.concatenate([np.cos(ang2), np.sin(ang2)], axis=2)
    to = lambda a: jnp.asarray(a, dtype=BF16)
    return to(chan), to(np.cos(ang1)), to(np.sin(ang1)), to(stage2)


FFT_ROWS = 8


def _fft_chan_kernel(f_ref, chan_ref, zr_ref, zi_ref):
    zr, zi = [], []
    for g in range(FNET_GROUPS):
        z = jnp.dot(f_ref[:, g * FNET_GROUP_DIM:(g + 1) * FNET_GROUP_DIM], chan_ref[...],
                    preferred_element_type=F32)
        zr.append(z[:, :FNET_GROUP_DIM])
        zi.append(z[:, FNET_GROUP_DIM:])
    zr_ref[...] = jnp.concatenate(zr, axis=1)
    zi_ref[...] = jnp.concatenate(zi, axis=1)


def _fft_a_kernel(zr_ref, zi_ref, c1_ref, s1_ref, ur_ref, ui_ref):
    c1 = c1_ref[...]
    s1 = s1_ref[...]
    dot = functools.partial(jnp.dot, preferred_element_type=F32)
    for j in range(FFT_ROWS):
        zr = zr_ref[:, j, :].astype(BF16)
        zi = zi_ref[:, j, :].astype(BF16)
        ur_ref[:, j, :] = dot(c1, zr) + dot(s1, zi)
        ui_ref[:, j, :] = dot(c1, zi) - dot(s1, zr)


def _fft_b_kernel(scale, ur_ref, ui_ref, m_ref, o_ref):
    for j in range(FFT_ROWS):
        u = jnp.concatenate([ur_ref[j], ui_ref[j]], axis=0).astype(BF16)
        o_ref[:, j, :] = jnp.dot(m_ref[j], u, preferred_element_type=F32) * scale


def _fft_mixer(proj, bsz, seq, tables):
    chan, c1, s1, stage2 = tables
    l1, l2 = _fft_split(seq)
    n = bsz * seq
    w = FNET_GROUPS * FNET_GROUP_DIM
    tm = min(1024, n)
    z_r, z_i = pl.pallas_call(
        _fft_chan_kernel,
        grid=(n // tm,),
        in_specs=[pl.BlockSpec((tm, w), lambda i: (i, COL_F // 4)),
                  pl.BlockSpec((FNET_GROUP_DIM, 2 * FNET_GROUP_DIM), lambda i: (0, 0))],
        out_specs=[pl.BlockSpec((tm, w), lambda i: (i, 0))] * 2,
        out_shape=[jax.ShapeDtypeStruct((n, w), F32)] * 2,
        compiler_params=_params("arbitrary"),
        name="fft_channels",
    )(proj, chan)
    slab = pl.BlockSpec((None, l1, FFT_ROWS, w), lambda b, t: (b, 0, t, 0))
    u_r, u_i = pl.pallas_call(
        _fft_a_kernel,
        grid=(bsz, l2 // FFT_ROWS),
        in_specs=[slab, slab, pl.BlockSpec((l1, l1), lambda b, t: (0, 0)), pl.BlockSpec((l1, l1), lambda b, t: (0, 0))],
        out_specs=[slab, slab],
        out_shape=[jax.ShapeDtypeStruct((bsz, l1, l2, w), F32)] * 2,
        compiler_params=_params("arbitrary", "arbitrary"),
        name="fft_stage_a",
    )(z_r.reshape(bsz, l1, l2, w), z_i.reshape(bsz, l1, l2, w), c1, s1)
    scale = 1.0 / math.sqrt(seq * FNET_GROUP_DIM)
    rows = pl.BlockSpec((None, FFT_ROWS, l2, w), lambda b, i: (b, i, 0, 0))
    y = pl.pallas_call(
        functools.partial(_fft_b_kernel, scale),
        grid=(bsz, l1 // FFT_ROWS),
        in_specs=[rows, rows, pl.BlockSpec((FFT_ROWS, l2, 2 * l2), lambda b, i: (i, 0, 0))],
        out_specs=pl.BlockSpec((None, l2, FFT_ROWS, w), lambda b, i: (b, 0, i, 0)),
        out_shape=jax.ShapeDtypeStruct((bsz, l2, l1, w), F32),
        compiler_params=_params("arbitrary", "arbitrary"),
        name="fft_stage_b",
    )(u_r, u_i, stage2)
    return y.reshape(n, w)


def _merge_kernel(x_ref, g0_ref, g1_ref, g2_ref, g3_ref, o0_ref, o1_ref, o2_ref, o3_ref,
                  wb_ref, wo_ref, nf_ref, wr_ref, xo_ref, u_ref, aff_ref):
    merged = None
    for kbr, (g_ref, o_ref) in enumerate(((g0_ref, o0_ref), (g1_ref, o1_ref), (g2_ref, o2_ref), (g3_ref, o3_ref))):
        term = jax.nn.sigmoid(g_ref[...].astype(F32)) * jnp.dot(o_ref[...].astype(BF16), wb_ref[kbr], preferred_element_type=F32)
        merged = term if merged is None else merged + term
    x = x_ref[...] + jnp.dot(merged.astype(BF16), wo_ref[...], preferred_element_type=F32)
    xo_ref[...] = x
    ms = jnp.mean(x * x, axis=-1, keepdims=True)
    u = (x * lax.rsqrt(ms + EPS) * nf_ref[...]).astype(BF16)
    u_ref[...] = u
    logits = jnp.dot(u, wr_ref[...], preferred_element_type=F32)
    lane = lax.broadcasted_iota(jnp.int32, logits.shape, 1)
    logits = jnp.where(lane < N_EXPERTS, logits, -jnp.inf)
    e = jnp.exp(logits - jnp.max(logits, axis=-1, keepdims=True))
    aff_ref[...] = e / jnp.sum(e, axis=-1, keepdims=True)


def _merge(x2, proj, branches, w_branch, w_out, norm_ffn, w_router, layer):
    n, d = x2.shape
    tm = min(256, n)
    bw = SSD_WIDTH
    gate_spec = lambda kbr: pl.BlockSpec((tm, d), lambda i: (i, COL_GATES // 8 + kbr))
    br_spec = pl.BlockSpec((tm, bw), lambda i: (i, 0))
    return pl.pallas_call(
        _merge_kernel,
        grid=(n // tm,),
        in_specs=[pl.BlockSpec((tm, d), lambda i: (i, 0))] + [gate_spec(k) for k in range(N_BRANCHES)]
                 + [br_spec] * N_BRANCHES
                 + [pl.BlockSpec((None, N_BRANCHES, bw, d), lambda i: (layer, 0, 0, 0)),
                    pl.BlockSpec((None, d, d), lambda i: (layer, 0, 0)),
                    pl.BlockSpec((None, 1, d), lambda i: (layer, 0, 0)),
                    pl.BlockSpec((None, d, LANES), lambda i: (layer, 0, 0))],
        out_specs=[pl.BlockSpec((tm, d), lambda i: (i, 0)), pl.BlockSpec((tm, d), lambda i: (i, 0)),
                   pl.BlockSpec((tm, LANES), lambda i: (i, 0))],
        out_shape=[jax.ShapeDtypeStruct((n, d), F32), jax.ShapeDtypeStruct((n, d), BF16),
                   jax.ShapeDtypeStruct((n, LANES), F32)],
        compiler_params=_params("arbitrary"),
        name="merge_out",
    )(x2, proj, proj, proj, proj, *branches, w_branch, w_out, norm_ffn, w_router)


def _ffn_kernel(xe_ref, gate_ref, wu_ref, wd_ref, o_ref):
    h = jnp.dot(xe_ref[...], wu_ref[...], preferred_element_type=F32)
    act = (_silu(h[:, :EXPERT_FF]) * h[:, EXPERT_FF:]).astype(BF16)
    o_ref[...] = jnp.dot(act, wd_ref[...], preferred_element_type=F32) * gate_ref[...]


def _expert_ffn(xe, gate, w_up, w_down, layer):
    e, cap, d = xe.shape
    tc = min(512, cap)
    return pl.pallas_call(
        _ffn_kernel,
        grid=(e, cap // tc),
        in_specs=[pl.BlockSpec((None, tc, d), lambda i, j: (i, j, 0)),
                  pl.BlockSpec((None, tc, 1), lambda i, j: (i, j, 0)),
                  pl.BlockSpec((None, None, d, 2 * EXPERT_FF), lambda i, j: (layer, i, 0, 0)),
                  pl.BlockSpec((None, None, EXPERT_FF, d), lambda i, j: (layer, i, 0, 0))],
        out_specs=pl.BlockSpec((None, tc, d), lambda i, j: (i, j, 0)),
        out_shape=jax.ShapeDtypeStruct((e, cap, d), F32),
        compiler_params=_params("arbitrary", "arbitrary"),
        name="expert_ffn",
    )(xe, gate, w_up, w_down)


COMBINE_TILE = 128
COMBINE_UNROLL = 4


def _combine_window_start(first_row, total_rows):
    return pl.multiple_of(jnp.minimum((first_row // COMBINE_TILE) * COMBINE_TILE, total_rows - 2 * COMBINE_TILE),
                          COMBINE_TILE)


def _combine_kernel(total_rows, tok_ref, r0_ref, x_ref, *refs):
    ye_refs, o_ref = refs[:N_EXPERTS], refs[N_EXPERTS]
    t = pl.program_id(0)
    tok0 = t * COMBINE_TILE
    o_ref[...] = x_ref[...]

    for e in range(N_EXPERTS):
        lo = r0_ref[t * N_EXPERTS + e]
        hi = r0_ref[(t + 1) * N_EXPERTS + e]
        buf_ref = ye_refs[e]
        row0 = _combine_window_start(lo, total_rows)

        def body_n(i, carry):
            r = lo + COMBINE_UNROLL * i
            toks = [tok_ref[r + j] - tok0 for j in range(COMBINE_UNROLL)]
            vals = [o_ref[pl.ds(toks[j], 1), :] + buf_ref[pl.ds(r + j - row0, 1), :] for j in range(COMBINE_UNROLL)]
            for j in range(COMBINE_UNROLL):
                o_ref[pl.ds(toks[j], 1), :] = vals[j]
            return carry

        n_full = (hi - lo) // COMBINE_UNROLL
        lax.fori_loop(0, n_full, body_n, 0)

        def body_1(r, carry):
            tk = tok_ref[r] - tok0
            o_ref[pl.ds(tk, 1), :] = o_ref[pl.ds(tk, 1), :] + buf_ref[pl.ds(r - row0, 1), :]
            return carry

        lax.fori_loop(lo + COMBINE_UNROLL * n_full, hi, body_1, 0)


def _combine(x2, ye, idx):
    n, d = x2.shape
    n_exp, cap = idx.shape
    nt = n // COMBINE_TILE
    total_rows = n_exp * cap
    bounds = jnp.arange(nt + 1, dtype=jnp.int32) * COMBINE_TILE
    r0 = jnp.sum((idx[:, :, None] < bounds[None, None, :]).astype(jnp.int32), axis=1)
    r0 = (r0 + (jnp.arange(n_exp, dtype=jnp.int32) * cap)[:, None]).T.reshape(-1)

    def ye_spec(e):
        return pl.BlockSpec((pl.Element(2 * COMBINE_TILE), pl.Element(d)),
                            lambda t, tok, r0s: (_combine_window_start(r0s[t * N_EXPERTS + e], total_rows), 0))

    grid_spec = pltpu.PrefetchScalarGridSpec(
        num_scalar_prefetch=2,
        grid=(nt,),
        in_specs=[pl.BlockSpec((COMBINE_TILE, d), lambda t, tok, r0s: (t, 0))] + [ye_spec(e) for e in range(n_exp)],
        out_specs=pl.BlockSpec((COMBINE_TILE, d), lambda t, tok, r0s: (t, 0)),
    )
    return pl.pallas_call(
        functools.partial(_combine_kernel, total_rows),
        grid_spec=grid_spec,
        out_shape=jax.ShapeDtypeStruct((n, d), x2.dtype),
        input_output_aliases={2: 0},
        compiler_params=_params("arbitrary"),
        name="expert_combine",
    )(idx.reshape(-1).astype(jnp.int32), r0, x2, *([ye] * n_exp))


def _norm_kernel(x_ref, w_ref, o_ref):
    x = x_ref[...]
    ms = jnp.mean(x * x, axis=-1, keepdims=True)
    o_ref[...] = x * lax.rsqrt(ms + EPS) * w_ref[...]


def _final_norm(x2, w):
    n, d = x2.shape
    tm = min(1024, n)
    return pl.pallas_call(
        _norm_kernel,
        grid=(n // tm,),
        in_specs=[pl.BlockSpec((tm, d), lambda i: (i, 0)), pl.BlockSpec((1, d), lambda i: (0, 0))],
        out_specs=pl.BlockSpec((tm, d), lambda i: (i, 0)),
        out_shape=jax.ShapeDtypeStruct((n, d), F32),
        compiler_params=_params("arbitrary"),
        name="final_norm",
    )(x2, w)


def _pad_lanes(v):
    return jnp.pad(v, (0, LANES - v.shape[0]))[None, :]


def _prepare(norm_mix, w_in, conv_w, conv_b, ssd_dt_bias, ssd_a_log, ssd_d, ssd_norm, ret_log2_decay,
             attn_q_norm, attn_k_norm, w_branch, w_out, norm_ffn, w_router, w_up, w_down, norm_final):
    p = dict(
        norm_mix=norm_mix[:, None, :],
        w_proj=_repack_proj_weights(w_in),
        conv_w=conv_w, conv_b=conv_b[:, None, :],
        dtb=[_pad_lanes(ssd_dt_bias[l].reshape(-1)) for l in range(DEPTH)],
        alog=[_pad_lanes(ssd_a_log[l].reshape(-1)) for l in range(DEPTH)],
        dsk=[jnp.repeat(ssd_d[l], SSD_HEAD_DIM)[None, :] for l in range(DEPTH)],
        ssd_norm=ssd_norm[:, None, :],
        log_gamma=jnp.log1p(-jnp.exp2(ret_log2_decay.astype(F32))).reshape(DEPTH, 2 * RET_HEADS),
        q_norm=attn_q_norm, k_norm=attn_k_norm,
        w_branch=w_branch.astype(BF16), w_out=w_out.astype(BF16),
        norm_ffn=norm_ffn[:, None, :],
        w_router=jnp.pad(w_router, ((0, 0), (0, 0), (0, LANES - N_EXPERTS))).astype(BF16),
        w_up=w_up.astype(BF16), w_down=w_down.astype(BF16),
        norm_final=norm_final[None, :],
    )
    qw = ATT_HEADS * ATT_HEAD_DIM
    hid = np.arange(qw) // ATT_HEAD_DIM
    p["ones_bd"] = jnp.asarray(hid[:, None] == hid[None, :], dtype=BF16)
    return p


def _trunk(x, p):
    bsz, seq, d = x.shape
    n = bsz * seq
    cap = (EC_CAPACITY_FACTOR * n) // N_EXPERTS
    ret_cos, ret_sin = _rotary_tables(seq, RET_DIM)
    att_cos, att_sin = [jnp.tile(t, (1, ATT_KV_HEADS)) for t in _rotary_tables(seq, ATT_HEAD_DIM)]
    fft_tables = _fft_tables(seq)
    x2 = x.reshape(n, d)
    for l in range(DEPTH):
        proj = _norm_proj(x2, p["norm_mix"], p["w_proj"], l)
        o_ssd = _ssd_mixer(proj, bsz, seq, p["conv_w"][l], p["conv_b"][l], p["dtb"][l], p["alog"][l], p["dsk"][l],
                           p["ssd_norm"][l])
        o_ret = _ret_mixer(proj, bsz, seq, p["log_gamma"][l], ret_cos, ret_sin)
        o_att = _att_mixer(proj, bsz, seq, p["q_norm"][l], p["k_norm"][l], att_cos, att_sin, p["ones_bd"])
        o_fft = _fft_mixer(proj, bsz, seq, fft_tables)
        x2, u, aff = _merge(x2, proj, (o_ssd, o_ret, o_att, o_fft), p["w_branch"], p["w_out"], p["norm_ffn"],
                            p["w_router"], l)
        gate, idx = lax.top_k(aff[:, :N_EXPERTS].T, cap)
        idx, gate = lax.sort_key_val(idx, gate, dimension=1)
        u_src = jnp.concatenate([u, u], axis=0) if n < 32768 else u
        xe = u_src.at[idx.reshape(-1)].get(mode="promise_in_bounds").reshape(N_EXPERTS, cap, d)
        ye = _expert_ffn(xe, gate[..., None], p["w_up"], p["w_down"], l)
        x2 = _combine(x2, ye.reshape(-1, d), idx)
    return _final_norm(x2, p["norm_final"]).reshape(bsz, seq, d)


def kernel(x_prompt, x_sample, norm_mix, w_in, conv_w, conv_b, ssd_dt_bias, ssd_a_log, ssd_d, ssd_norm, ret_log2_decay, attn_q_norm, attn_k_norm, w_branch, w_out, norm_ffn, w_router, w_up, w_down, norm_final):
    p = _prepare(norm_mix, w_in, conv_w, conv_b, ssd_dt_bias, ssd_a_log, ssd_d, ssd_norm, ret_log2_decay,
                 attn_q_norm, attn_k_norm, w_branch, w_out, norm_ffn, w_router, w_up, w_down, norm_final)
    return (_trunk(x_prompt, p), _trunk(x_sample, p))
```

```python
import functools
import math

import numpy as np
import jax
import jax.numpy as jnp
from jax import lax
from jax.experimental import pallas as pl
from jax.experimental.pallas import tpu as pltpu

F32 = jnp.float32
BF16 = jnp.bfloat16

D_MODEL = 1024
DEPTH = 4
GRID_W = 64
CHUNK = 128
EPS = 1e-6
ROPE_THETA = 10000.0

SSD_HEADS = 8
SSD_HEAD_DIM = 64
SSD_WIDTH = 512
SSD_GROUPS = 2
SSD_STATE = 128
SSD_CONV = 5
SSD_XBC = 1024

RET_HEADS = 4
RET_DIM = 128

ATT_HEADS = 8
ATT_KV_HEADS = 2
ATT_HEAD_DIM = 64

FNET_GROUPS = 4
FNET_GROUP_DIM = 128

N_BRANCHES = 4
N_EXPERTS = 16
EXPERT_FF = 1024
EC_CAPACITY_FACTOR = 2

IN_SIZES = (512, 1024, 8, 8, 512, 512, 512, 512, 512, 128, 128, 512, 4096)
D_IN = sum(IN_SIZES)

LANES = 128
CONV_HALO = 16

COL_XBC, COL_Z, COL_RQ, COL_RK, COL_RV, COL_RG, COL_AQ = 0, 8, 12, 16, 20, 24, 28
COL_GATES, COL_F, COL_AK, COL_AV, COL_DT = 32, 64, 68, 69, 70
PROJ_COLS = 72 * LANES


def _repack_proj_weights(w_in):
    offs = np.concatenate([[0], np.cumsum(IN_SIZES)])
    (z, xbc, dtf, dtb, rq, rk, rv, rg, aq, ak, av, f, gates) = [w_in[..., offs[i]:offs[i + 1]] for i in range(13)]
    pad = jnp.zeros(w_in.shape[:-1] + (2 * LANES - 16,), w_in.dtype)
    w = jnp.concatenate([xbc, z, rq, rk, rv, rg, aq, gates, f, ak, av, dtf, dtb, pad], axis=-1)
    assert w.shape[-1] == PROJ_COLS
    return w.astype(BF16)


def _params(*sem):
    return pltpu.CompilerParams(dimension_semantics=sem, vmem_limit_bytes=56 * 1024 * 1024)


def _silu(x):
    return x * jax.nn.sigmoid(x)


def _proj_kernel(x_ref, nw_ref, w_ref, o_ref, u_scr):
    @pl.when(pl.program_id(1) == 0)
    def _():
        x = x_ref[...]
        ms = jnp.mean(x * x, axis=-1, keepdims=True)
        u_scr[...] = (x * lax.rsqrt(ms + EPS) * nw_ref[...]).astype(BF16)

    o_ref[...] = jnp.dot(u_scr[...], w_ref[...], preferred_element_type=F32).astype(o_ref.dtype)


def _norm_proj(x2, nw, w_all, layer):
    n, d = x2.shape
    cols = w_all.shape[-1]
    tm = min(1024, n)
    tn = 3072
    return pl.pallas_call(
        _proj_kernel,
        grid=(n // tm, cols // tn),
        in_specs=[pl.BlockSpec((tm, d), lambda i, j: (i, 0)),
                  pl.BlockSpec((None, 1, d), lambda i, j: (layer, 0, 0)),
                  pl.BlockSpec((None, d, tn), lambda i, j: (layer, 0, j))],
        out_specs=pl.BlockSpec((tm, tn), lambda i, j: (i, j)),
        out_shape=jax.ShapeDtypeStruct((n, cols), BF16),
        scratch_shapes=[pltpu.VMEM((tm, d), BF16)],
        compiler_params=_params("arbitrary", "arbitrary"),
        name="norm_proj",
    )(x2, nw, w_all)


def _cumsum_rows(x, reverse):
    n = x.shape[0]
    row = lax.broadcasted_iota(jnp.int32, x.shape, 0)
    d = 1
    while d < n:
        if reverse:
            x = x + jnp.where(row < n - d, pltpu.roll(x, n - d, axis=0), 0.0)
        else:
            x = x + jnp.where(row >= d, pltpu.roll(x, d, axis=0), 0.0)
        d *= 2
    return x


def _softplus(x):
    return jnp.maximum(x, 0.0) + jnp.log1p(jnp.exp(-jnp.abs(x)))


def _ssd_kernel(rev, *refs):
    if rev:
        (act_ref, dt_ref, dtb_ref, alog_ref, z_ref, yf_ref, nw_ref, o_ref, h_scr) = refs
    else:
        (prev_ref, main_ref, next_ref, dt_ref, cw_ref, cb_ref, dtb_ref, alog_ref, dsk_ref,
         o_ref, act_ref, ext_scr, h_scr) = refs
    c = pl.program_id(1)
    nc = pl.num_programs(1)
    cc = nc - 1 - c if rev else c

    @pl.when(c == 0)
    def _():
        h_scr[...] = jnp.zeros_like(h_scr)

    if rev:
        act = act_ref[...].astype(F32)
    else:
        ext_scr[0:CONV_HALO, :] = jnp.where(cc == 0, 0.0, prev_ref[...].astype(F32))
        ext_scr[CONV_HALO:CONV_HALO + CHUNK, :] = main_ref[...].astype(F32)
        ext_scr[CONV_HALO + CHUNK:, :] = jnp.where(cc == nc - 1, 0.0, next_ref[...].astype(F32))
        acc = jnp.broadcast_to(cb_ref[...], (CHUNK, SSD_XBC))
        for k in range(SSD_CONV):
            acc = acc + cw_ref[k:k + 1, :] * ext_scr[pl.ds(CONV_HALO - SSD_CONV // 2 + k, CHUNK), :]
        act = _silu(acc)
        act_ref[...] = act.astype(act_ref.dtype)
    gn = SSD_GROUPS * SSD_STATE
    x = act[:, :SSD_WIDTH]
    b_mat = act[:, SSD_WIDTH:SSD_WIDTH + gn]
    c_mat = act[:, SSD_WIDTH + gn:]

    off = SSD_HEADS if rev else 0
    dt = _softplus(dt_ref[...].astype(F32) + dtb_ref[...])
    a = -jnp.exp(alog_ref[...])
    cs = _cumsum_rows(dt * a, rev)
    cs_t = cs.T
    rowi = lax.broadcasted_iota(jnp.int32, (CHUNK, CHUNK), 0)
    coli = lax.broadcasted_iota(jnp.int32, (CHUNK, CHUNK), 1)
    mask = (rowi < coli) if rev else (rowi >= coli)

    def expand(v):
        return jnp.concatenate(
            [jnp.broadcast_to(v[:, off + h:off + h + 1], (CHUNK, SSD_HEAD_DIM)) for h in range(SSD_HEADS)], axis=1)

    dt_e = expand(dt)
    cs_e = expand(cs)
    edge = cs_e[0:1, :] if rev else cs_e[CHUNK - 1:CHUNK, :]
    xdt = x * dt_e
    xst = (xdt * jnp.exp(edge - cs_e)).astype(BF16)
    xdt = xdt.astype(BF16)
    e_in = jnp.exp(cs_e)
    e_chunk = jnp.exp(edge)

    r = SSD_HEADS // SSD_GROUPS
    gw = r * SSD_HEAD_DIM
    ys = []
    for g in range(SSD_GROUPS):
        bg = b_mat[:, g * SSD_STATE:(g + 1) * SSD_STATE].astype(BF16)
        cg = c_mat[:, g * SSD_STATE:(g + 1) * SSD_STATE].astype(BF16)
        scores = lax.dot_general(cg, bg, (((1,), (1,)), ((), ())), preferred_element_type=F32)
        h_prev = h_scr[g]
        y_off = jnp.dot(cg, h_prev.astype(BF16), preferred_element_type=F32) * e_in[:, g * gw:(g + 1) * gw]
        st = lax.dot_general(bg, xst[:, g * gw:(g + 1) * gw], (((0,), (0,)), ((), ())),
                             preferred_element_type=F32)
        h_scr[g] = h_prev * e_chunk[:, g * gw:(g + 1) * gw] + st
        for j in range(r):
            h = g * r + j
            colb = jnp.broadcast_to(cs[:, off + h:off + h + 1], (CHUNK, CHUNK))
            rowb = cs_t[off + h:off + h + 1, :]
            decay = jnp.exp(jnp.where(mask, colb - rowb, -jnp.inf))
            m = (scores * decay).astype(BF16)
            y_d = jnp.dot(m, xdt[:, h * SSD_HEAD_DIM:(h + 1) * SSD_HEAD_DIM], preferred_element_type=F32)
            ys.append(y_d + y_off[:, j * SSD_HEAD_DIM:(j + 1) * SSD_HEAD_DIM])
    y = jnp.concatenate(ys, axis=1)

    if rev:
        y = (y + yf_ref[...]) * _silu(z_ref[...].astype(F32))
        ms = jnp.mean(y * y, axis=-1, keepdims=True)
        o_ref[...] = (y * lax.rsqrt(ms + EPS) * nw_ref[...]).astype(o_ref.dtype)
    else:
        o_ref[...] = y + x * dsk_ref[...]


def _ssd_mixer(proj, bsz, seq, conv_w, conv_b, dtb_row, alog_row, dsk_row, norm_w):
    n = bsz * seq
    nc = seq // CHUNK
    hb = CHUNK // CONV_HALO
    nhalo = n // CONV_HALO

    state = pltpu.VMEM((SSD_GROUPS, SSD_STATE, SSD_WIDTH // SSD_GROUPS), F32)
    const = lambda shape: pl.BlockSpec(shape, lambda b, c: (0, 0))
    row = lambda b, c: b * nc + c
    y_f, act = pl.pallas_call(
        functools.partial(_ssd_kernel, False),
        grid=(bsz, nc),
        in_specs=[
            pl.BlockSpec((CONV_HALO, SSD_XBC), lambda b, c: (jnp.maximum(row(b, c) * hb - 1, 0), COL_XBC // 8)),
            pl.BlockSpec((CHUNK, SSD_XBC), lambda b, c: (row(b, c), COL_XBC // 8)),
            pl.BlockSpec((CONV_HALO, SSD_XBC), lambda b, c: (jnp.minimum((row(b, c) + 1) * hb, nhalo - 1), COL_XBC // 8)),
            pl.BlockSpec((CHUNK, LANES), lambda b, c: (row(b, c), COL_DT)),
            const((SSD_CONV, SSD_XBC)), const((1, SSD_XBC)), const((1, LANES)), const((1, LANES)), const((1, SSD_WIDTH)),
        ],
        out_specs=[pl.BlockSpec((CHUNK, SSD_WIDTH), lambda b, c: (row(b, c), 0)),
                   pl.BlockSpec((CHUNK, SSD_XBC), lambda b, c: (row(b, c), 0))],
        out_shape=[jax.ShapeDtypeStruct((n, SSD_WIDTH), F32), jax.ShapeDtypeStruct((n, SSD_XBC), BF16)],
        scratch_shapes=[pltpu.VMEM((CHUNK + 2 * CONV_HALO, SSD_XBC), F32), state],
        compiler_params=_params("arbitrary", "arbitrary"),
        name="ssd_fwd",
    )(proj, proj, proj, proj, conv_w, conv_b, dtb_row, alog_row, dsk_row)

    rrow = lambda b, c: b * nc + (nc - 1 - c)
    return pl.pallas_call(
        functools.partial(_ssd_kernel, True),
        grid=(bsz, nc),
        in_specs=[
            pl.BlockSpec((CHUNK, SSD_XBC), lambda b, c: (rrow(b, c), 0)),
            pl.BlockSpec((CHUNK, LANES), lambda b, c: (rrow(b, c), COL_DT)),
            const((1, LANES)), const((1, LANES)),
            pl.BlockSpec((CHUNK, SSD_WIDTH), lambda b, c: (rrow(b, c), COL_Z // 4)),
            pl.BlockSpec((CHUNK, SSD_WIDTH), lambda b, c: (rrow(b, c), 0)),
            const((1, SSD_WIDTH)),
        ],
        out_specs=pl.BlockSpec((CHUNK, SSD_WIDTH), lambda b, c: (rrow(b, c), 0)),
        out_shape=jax.ShapeDtypeStruct((n, SSD_WIDTH), BF16),
        scratch_shapes=[state],
        compiler_params=_params("arbitrary", "arbitrary"),
        name="ssd_bwd",
    )(act, proj, dtb_row, alog_row, proj, y_f, norm_w)


def _rotary_tables(seq, head_dim):
    n_freq = head_dim // 4
    freqs = ROPE_THETA ** (-jnp.arange(n_freq, dtype=F32) / n_freq)
    t = jnp.arange(seq)
    ang_row = (t // GRID_W).astype(F32)[:, None] * freqs
    ang_col = (t % GRID_W).astype(F32)[:, None] * freqs
    cos = jnp.concatenate([jnp.cos(ang_row)] * 2 + [jnp.cos(ang_col)] * 2, axis=1)
    sin = jnp.concatenate([-jnp.sin(ang_row), jnp.sin(ang_row), -jnp.sin(ang_col), jnp.sin(ang_col)], axis=1)
    return cos, sin


def _rotate(x, cos, sin_signed, quarter):
    w = x.shape[1]
    lane = lax.broadcasted_iota(jnp.int32, x.shape, 1)
    partner = jnp.where((lane % (2 * quarter)) < quarter,
                        pltpu.roll(x, w - quarter, axis=1), pltpu.roll(x, quarter, axis=1))
    return x * cos + partner * sin_signed


def _ret_kernel(rev, *refs):
    if rev:
        (lg_ref, q_ref, k_ref, v_ref, cos_ref, sin_ref, g_ref, yf_ref, o_ref, tab_scr, h_scr) = refs
    else:
        (lg_ref, q_ref, k_ref, v_ref, cos_ref, sin_ref, o_ref, tab_scr, h_scr) = refs
    first = jnp.logical_and(pl.program_id(0) == 0, pl.program_id(1) == 0)

    @pl.when(first)
    def _():
        rowi = lax.broadcasted_iota(jnp.int32, (CHUNK, CHUNK), 0)
        coli = lax.broadcasted_iota(jnp.int32, (CHUNK, CHUNK), 1)
        rowf = rowi.astype(F32)
        for h in range(RET_HEADS):
            lg = lg_ref[(RET_HEADS if rev else 0) + h]
            if rev:
                dist = jnp.where(coli > rowi, (coli - rowi).astype(F32) * lg, -jnp.inf)
                kd = rowf * lg
                qd = (CHUNK - rowf) * lg
            else:
                dist = jnp.where(rowi >= coli, (rowi - coli).astype(F32) * lg, -jnp.inf)
                kd = (CHUNK - 1 - rowf) * lg
                qd = (rowf + 1.0) * lg
            tab_scr[0, h] = jnp.exp(dist)
            tab_scr[1, h] = jnp.exp(kd)
            tab_scr[2, h] = jnp.exp(qd)
            tab_scr[3, h] = jnp.exp(jnp.full((CHUNK, CHUNK), CHUNK, F32) * lg)

    @pl.when(pl.program_id(1) == 0)
    def _():
        h_scr[...] = jnp.zeros_like(h_scr)

    cos = jnp.concatenate([cos_ref[...]] * RET_HEADS, axis=1)
    sin = jnp.concatenate([sin_ref[...]] * RET_HEADS, axis=1)
    q = _rotate(q_ref[...].astype(F32), cos, sin, RET_DIM // 4)
    k = _rotate(k_ref[...].astype(F32), cos, sin, RET_DIM // 4) * (RET_DIM ** -0.5)
    v = v_ref[...].astype(F32)
    ys = []
    for h in range(RET_HEADS):
        sl = slice(h * RET_DIM, (h + 1) * RET_DIM)
        qh = q[:, sl].astype(BF16)
        kh = k[:, sl].astype(BF16)
        vh = v[:, sl]
        s = lax.dot_general(qh, kh, (((1,), (1,)), ((), ())), preferred_element_type=F32) * tab_scr[0, h]
        inner = jnp.dot(s.astype(BF16), vh.astype(BF16), preferred_element_type=F32)
        h_prev = h_scr[h]
        cross = jnp.dot(qh, h_prev.astype(BF16), preferred_element_type=F32) * tab_scr[2, h]
        st = lax.dot_general(kh, (vh * tab_scr[1, h]).astype(BF16), (((0,), (0,)), ((), ())),
                             preferred_element_type=F32)
        h_scr[h] = h_prev * tab_scr[3, h] + st
        ys.append(inner + cross)

    if rev:
        outs = []
        for h in range(RET_HEADS):
            sl = slice(h * RET_DIM, (h + 1) * RET_DIM)
            y = ys[h] + yf_ref[:, sl]
            y = y * lax.rsqrt(jnp.mean(y * y, axis=-1, keepdims=True) + EPS)
            outs.append(y * _silu(g_ref[:, sl].astype(F32)))
        o_ref[...] = jnp.concatenate(outs, axis=1).astype(o_ref.dtype)
    else:
        o_ref[...] = jnp.concatenate(ys, axis=1)


def _ret_mixer(proj, bsz, seq, log_gamma, cos, sin):
    n = bsz * seq
    nc = seq // CHUNK
    width = RET_HEADS * RET_DIM

    def call(rev, extra_in, extra_cols, out_dtype):
        chunk = (lambda c: nc - 1 - c) if rev else (lambda c: c)
        row = lambda b, c: b * nc + chunk(c)
        pcol = lambda cb: pl.BlockSpec((CHUNK, width), lambda b, c: (row(b, c), cb))
        tab = pl.BlockSpec((CHUNK, RET_DIM), lambda b, c: (chunk(c), 0))
        in_specs = [pl.BlockSpec(memory_space=pltpu.SMEM),
                    pcol(COL_RQ // 4), pcol(COL_RK // 4), pcol(COL_RV // 4), tab, tab] + [pcol(cb) for cb in extra_cols]
        return pl.pallas_call(
            functools.partial(_ret_kernel, rev),
            grid=(bsz, nc),
            in_specs=in_specs,
            out_specs=pl.BlockSpec((CHUNK, width), lambda b, c: (row(b, c), 0)),
            out_shape=jax.ShapeDtypeStruct((n, width), out_dtype),
            scratch_shapes=[pltpu.VMEM((4, RET_HEADS, CHUNK, CHUNK), F32),
                            pltpu.VMEM((RET_HEADS, RET_DIM, RET_DIM), F32)],
            compiler_params=_params("arbitrary", "arbitrary"),
            name="ret_bwd" if rev else "ret_fwd",
        )(log_gamma, proj, proj, proj, cos, sin, *extra_in)

    y_f = call(False, (), (), F32)
    return call(True, (proj, y_f), (COL_RG // 4, 0), BF16)


def _head_mean_square(x, ones_bd):
    x2 = x * x
    hi = x2.astype(BF16)
    lo = (x2 - hi.astype(F32)).astype(BF16)
    tot = jnp.dot(hi, ones_bd, preferred_element_type=F32) + jnp.dot(lo, ones_bd, preferred_element_type=F32)
    return tot * (1.0 / ATT_HEAD_DIM)


def _pad_heads(x, n_heads, fill=None):
    pad = jnp.zeros((x.shape[0], LANES - ATT_HEAD_DIM), x.dtype) if fill is None else fill
    parts = []
    for h in range(n_heads):
        parts += [x[:, h * ATT_HEAD_DIM:(h + 1) * ATT_HEAD_DIM], pad]
    return jnp.concatenate(parts, axis=1)


def _att_prep_kernel(q_ref, k_ref, v_ref, qn_ref, kn_ref, cos_ref, sin_ref, ones_ref, qo_ref, ko_ref, vo_ref):
    rep = ATT_HEADS // ATT_KV_HEADS
    cos1 = cos_ref[...]
    sin1 = sin_ref[...]
    q = q_ref[...].astype(F32)
    q = q * lax.rsqrt(_head_mean_square(q, ones_ref[...]) + EPS) * qn_ref[...]
    q = _rotate(q, jnp.concatenate([cos1] * rep, axis=1), jnp.concatenate([sin1] * rep, axis=1), ATT_HEAD_DIM // 4)
    qo_ref[...] = _pad_heads((q * (ATT_HEAD_DIM ** -0.5)).astype(qo_ref.dtype), ATT_HEADS)
    k = k_ref[...].astype(F32)
    kw = ATT_KV_HEADS * ATT_HEAD_DIM
    k = k * lax.rsqrt(_head_mean_square(k, ones_ref[0:kw, 0:kw]) + EPS) * kn_ref[...]
    k = _rotate(k, cos1, sin1, ATT_HEAD_DIM // 4)
    ko_ref[...] = _pad_heads(k.astype(ko_ref.dtype), ATT_KV_HEADS)
    v = v_ref[...]
    vo_ref[...] = _pad_heads(v, ATT_KV_HEADS, fill=jnp.ones((v.shape[0], LANES - ATT_HEAD_DIM), v.dtype))


def _flash_kernel(tq, tk, q_ref, k_ref, v_ref, o_ref, m_scr, acc_scr):
    ki = pl.program_id(2)
    rep = ATT_HEADS // ATT_KV_HEADS

    @pl.when(ki == 0)
    def _():
        m_scr[...] = jnp.full_like(m_scr, -jnp.inf)
        acc_scr[...] = jnp.zeros_like(acc_scr)

    for g in range(ATT_KV_HEADS):
        qs = jnp.concatenate([q_ref[:, (rep * g + j) * LANES:(rep * g + j + 1) * LANES] for j in range(rep)], axis=0)
        kg = k_ref[:, g * LANES:(g + 1) * LANES]
        vg = v_ref[:, g * LANES:(g + 1) * LANES]
        s = lax.dot_general(qs, kg, (((1,), (1,)), ((), ())), preferred_element_type=F32)
        m_prev = m_scr[g]
        m_new = jnp.maximum(m_prev, jnp.max(s, axis=1, keepdims=True))
        p = jnp.exp(s - jnp.tile(m_new, (1, tk // LANES)))
        alpha = jnp.exp(m_prev - m_new)
        acc_scr[g] = alpha * acc_scr[g] + jnp.dot(p.astype(BF16), vg, preferred_element_type=F32)
        m_scr[g] = m_new

    @pl.when(ki == pl.num_programs(2) - 1)
    def _():
        outs = []
        for g in range(ATT_KV_HEADS):
            for j in range(rep):
                rows = acc_scr[g, j * tq:(j + 1) * tq, :]
                outs.append(rows[:, :ATT_HEAD_DIM] / rows[:, ATT_HEAD_DIM:])
        o_ref[...] = jnp.concatenate(outs, axis=1).astype(o_ref.dtype)


def _att_mixer(proj, bsz, seq, q_norm, k_norm, cos, sin, ones_bd):
    n = bsz * seq
    qw = ATT_HEADS * ATT_HEAD_DIM
    kw = ATT_KV_HEADS * ATT_HEAD_DIM
    qpw = ATT_HEADS * LANES
    kpw = ATT_KV_HEADS * LANES
    tm = min(512, seq)
    nt = seq // tm
    qn_row = jnp.tile(q_norm, ATT_HEADS)[None, :]
    kn_row = jnp.tile(k_norm, ATT_KV_HEADS)[None, :]
    q_pad, k_pad, v_ext = pl.pallas_call(
        _att_prep_kernel,
        grid=(bsz, nt),
        in_specs=[pl.BlockSpec((tm, qw), lambda b, i: (b * nt + i, COL_AQ // 4)),
                  pl.BlockSpec((tm, kw), lambda b, i: (b * nt + i, COL_AK)),
                  pl.BlockSpec((tm, kw), lambda b, i: (b * nt + i, COL_AV)),
                  pl.BlockSpec((1, qw), lambda b, i: (0, 0)),
                  pl.BlockSpec((1, kw), lambda b, i: (0, 0)),
                  pl.BlockSpec((tm, kw), lambda b, i: (i, 0)),
                  pl.BlockSpec((tm, kw), lambda b, i: (i, 0)),
                  pl.BlockSpec((qw, qw), lambda b, i: (0, 0))],
        out_specs=[pl.BlockSpec((tm, qpw), lambda b, i: (b * nt + i, 0)),
                   pl.BlockSpec((tm, kpw), lambda b, i: (b * nt + i, 0)),
                   pl.BlockSpec((tm, kpw), lambda b, i: (b * nt + i, 0))],
        out_shape=[jax.ShapeDtypeStruct((n, qpw), BF16), jax.ShapeDtypeStruct((n, kpw), BF16),
                   jax.ShapeDtypeStruct((n, kpw), BF16)],
        compiler_params=_params("arbitrary", "arbitrary"),
        name="att_prep",
    )(proj, proj, proj, qn_row, kn_row, cos, sin, ones_bd)

    tq = min(512, seq)
    tk = min(2048, seq)
    nq = seq // tq
    nk = seq // tk
    rep = ATT_HEADS // ATT_KV_HEADS
    return pl.pallas_call(
        functools.partial(_flash_kernel, tq, tk),
        grid=(bsz, nq, nk),
        in_specs=[pl.BlockSpec((tq, qpw), lambda b, i, j: (b * nq + i, 0)),
                  pl.BlockSpec((tk, kpw), lambda b, i, j: (b * nk + j, 0)),
                  pl.BlockSpec((tk, kpw), lambda b, i, j: (b * nk + j, 0))],
        out_specs=pl.BlockSpec((tq, qw), lambda b, i, j: (b * nq + i, 0)),
        out_shape=jax.ShapeDtypeStruct((n, qw), BF16),
        scratch_shapes=[pltpu.VMEM((ATT_KV_HEADS, rep * tq, LANES), F32),
                        pltpu.VMEM((ATT_KV_HEADS, rep * tq, LANES), F32)],
        compiler_params=_params("arbitrary", "arbitrary", "arbitrary"),
        name="flash_attention",
    )(q_pad, k_pad, v_ext)


def _fft_split(seq):
    l1 = 1 << (int(math.log2(seq)) // 2)
    return l1, seq // l1


def _fft_tables(seq):
    l1, l2 = _fft_split(seq)
    c = np.arange(FNET_GROUP_DIM)
    ang = 2.0 * np.pi * ((c[:, None] * c[None, :]) % FNET_GROUP_DIM) / FNET_GROUP_DIM
    chan = np.concatenate([np.cos(ang), -np.sin(ang)], axis=1)
    k1 = np.arange(l1)
    ang1 = 2.0 * np.pi * ((k1[:, None] * k1[None, :]) % l1) / l1
    t2 = np.arange(l2)
    k2 = np.arange(l2)
    kk = k1[:, None, None] + l1 * k2[None, :, None]
    ang2 = 2.0 * np.pi * ((kk * t2[None, None, :]) % seq) / seq
    stage2 = np.concatenate([np.cos(ang2), np.sin(ang2)], axis=2)
    to = lambda a: jnp.asarray(a, dtype=BF16)
    stage1 = np.block([[np.cos(ang1), np.sin(ang1)], [-np.sin(ang1), np.cos(ang1)]])
    return to(chan), to(stage1), to(stage2)


FFT_ROWS = 8


def _fft_chan_kernel(f_ref, chan_ref, zr_ref, zi_ref):
    zr, zi = [], []
    for g in range(FNET_GROUPS):
        z = jnp.dot(f_ref[:, g * FNET_GROUP_DIM:(g + 1) * FNET_GROUP_DIM], chan_ref[...],
                    preferred_element_type=F32)
        zr.append(z[:, :FNET_GROUP_DIM])
        zi.append(z[:, FNET_GROUP_DIM:])
    zr_ref[...] = jnp.concatenate(zr, axis=1)
    zi_ref[...] = jnp.concatenate(zi, axis=1)


def _fft_a_kernel(zr_ref, zi_ref, w1_ref, ur_ref, ui_ref):
    w1 = w1_ref[...]
    l1 = w1.shape[0] // 2
    for j in range(FFT_ROWS):
        z = jnp.concatenate([zr_ref[:, j, :], zi_ref[:, j, :]], axis=0).astype(BF16)
        u = jnp.dot(w1, z, preferred_element_type=F32)
        ur_ref[:, j, :] = u[:l1]
        ui_ref[:, j, :] = u[l1:]


def _fft_b_kernel(scale, ur_ref, ui_ref, m_ref, o_ref):
    for j in range(FFT_ROWS):
        u = jnp.concatenate([ur_ref[j], ui_ref[j]], axis=0).astype(BF16)
        o_ref[:, j, :] = jnp.dot(m_ref[j], u, preferred_element_type=F32) * scale


def _fft_mixer(proj, bsz, seq, tables):
    chan, stage1, stage2 = tables
    l1, l2 = _fft_split(seq)
    n = bsz * seq
    w = FNET_GROUPS * FNET_GROUP_DIM
    tm = min(1024, n)
    z_r, z_i = pl.pallas_call(
        _fft_chan_kernel,
        grid=(n // tm,),
        in_specs=[pl.BlockSpec((tm, w), lambda i: (i, COL_F // 4)),
                  pl.BlockSpec((FNET_GROUP_DIM, 2 * FNET_GROUP_DIM), lambda i: (0, 0))],
        out_specs=[pl.BlockSpec((tm, w), lambda i: (i, 0))] * 2,
        out_shape=[jax.ShapeDtypeStruct((n, w), F32)] * 2,
        compiler_params=_params("arbitrary"),
        name="fft_channels",
    )(proj, chan)
    slab = pl.BlockSpec((None, l1, FFT_ROWS, w), lambda b, t: (b, 0, t, 0))
    u_r, u_i = pl.pallas_call(
        _fft_a_kernel,
        grid=(bsz, l2 // FFT_ROWS),
        in_specs=[slab, slab, pl.BlockSpec((2 * l1, 2 * l1), lambda b, t: (0, 0))],
        out_specs=[slab, slab],
        out_shape=[jax.ShapeDtypeStruct((bsz, l1, l2, w), F32)] * 2,
        compiler_params=_params("arbitrary", "arbitrary"),
        name="fft_stage_a",
    )(z_r.reshape(bsz, l1, l2, w), z_i.reshape(bsz, l1, l2, w), stage1)
    scale = 1.0 / math.sqrt(seq * FNET_GROUP_DIM)
    rows = pl.BlockSpec((None, FFT_ROWS, l2, w), lambda b, i: (b, i, 0, 0))
    y = pl.pallas_call(
        functools.partial(_fft_b_kernel, scale),
        grid=(bsz, l1 // FFT_ROWS),
        in_specs=[rows, rows, pl.BlockSpec((FFT_ROWS, l2, 2 * l2), lambda b, i: (i, 0, 0))],
        out_specs=pl.BlockSpec((None, l2, FFT_ROWS, w), lambda b, i: (b, 0, i, 0)),
        out_shape=jax.ShapeDtypeStruct((bsz, l2, l1, w), F32),
        compiler_params=_params("arbitrary", "arbitrary"),
        name="fft_stage_b",
    )(u_r, u_i, stage2)
    return y.reshape(n, w)


def _merge_kernel(x_ref, g0_ref, g1_ref, g2_ref, g3_ref, o0_ref, o1_ref, o2_ref, o3_ref,
                  wb_ref, wo_ref, nf_ref, wr_ref, xo_ref, u_ref, aff_ref):
    merged = None
    for kbr, (g_ref, o_ref) in enumerate(((g0_ref, o0_ref), (g1_ref, o1_ref), (g2_ref, o2_ref), (g3_ref, o3_ref))):
        term = jax.nn.sigmoid(g_ref[...].astype(F32)) * jnp.dot(o_ref[...].astype(BF16), wb_ref[kbr], preferred_element_type=F32)
        merged = term if merged is None else merged + term
    x = x_ref[...] + jnp.dot(merged.astype(BF16), wo_ref[...], preferred_element_type=F32)
    xo_ref[...] = x
    ms = jnp.mean(x * x, axis=-1, keepdims=True)
    u = (x * lax.rsqrt(ms + EPS) * nf_ref[...]).astype(BF16)
    u_ref[...] = u
    logits = jnp.dot(u, wr_ref[...], preferred_element_type=F32)
    lane = lax.broadcasted_iota(jnp.int32, logits.shape, 1)
    logits = jnp.where(lane < N_EXPERTS, logits, -jnp.inf)
    e = jnp.exp(logits - jnp.max(logits, axis=-1, keepdims=True))
    aff_ref[...] = e / jnp.sum(e, axis=-1, keepdims=True)


def _merge(x2, proj, branches, w_branch, w_out, norm_ffn, w_router, layer):
    n, d = x2.shape
    tm = min(256, n)
    bw = SSD_WIDTH
    gate_spec = lambda kbr: pl.BlockSpec((tm, d), lambda i: (i, COL_GATES // 8 + kbr))
    br_spec = pl.BlockSpec((tm, bw), lambda i: (i, 0))
    return pl.pallas_call(
        _merge_kernel,
        grid=(n // tm,),
        in_specs=[pl.BlockSpec((tm, d), lambda i: (i, 0))] + [gate_spec(k) for k in range(N_BRANCHES)]
                 + [br_spec] * N_BRANCHES
                 + [pl.BlockSpec((None, N_BRANCHES, bw, d), lambda i: (layer, 0, 0, 0)),
                    pl.BlockSpec((None, d, d), lambda i: (layer, 0, 0)),
                    pl.BlockSpec((None, 1, d), lambda i: (layer, 0, 0)),
                    pl.BlockSpec((None, d, LANES), lambda i: (layer, 0, 0))],
        out_specs=[pl.BlockSpec((tm, d), lambda i: (i, 0)), pl.BlockSpec((tm, d), lambda i: (i, 0)),
                   pl.BlockSpec((tm, LANES), lambda i: (i, 0))],
        out_shape=[jax.ShapeDtypeStruct((n, d), F32), jax.ShapeDtypeStruct((n, d), BF16),
                   jax.ShapeDtypeStruct((n, LANES), F32)],
        compiler_params=_params("arbitrary"),
        name="merge_out",
    )(x2, proj, proj, proj, proj, *branches, w_branch, w_out, norm_ffn, w_router)


def _ffn_kernel(xe_ref, gate_ref, wu_ref, wd_ref, o_ref):
    h = jnp.dot(xe_ref[...], wu_ref[...], preferred_element_type=F32)
    act = (_silu(h[:, :EXPERT_FF]) * h[:, EXPERT_FF:]).astype(BF16)
    o_ref[...] = jnp.dot(act, wd_ref[...], preferred_element_type=F32) * gate_ref[...]


def _expert_ffn(xe, gate, w_up, w_down, layer):
    e, cap, d = xe.shape
    tc = min(512, cap)
    return pl.pallas_call(
        _ffn_kernel,
        grid=(e, cap // tc),
        in_specs=[pl.BlockSpec((None, tc, d), lambda i, j: (i, j, 0)),
                  pl.BlockSpec((None, tc, 1), lambda i, j: (i, j, 0)),
                  pl.BlockSpec((None, None, d, 2 * EXPERT_FF), lambda i, j: (layer, i, 0, 0)),
                  pl.BlockSpec((None, None, EXPERT_FF, d), lambda i, j: (layer, i, 0, 0))],
        out_specs=pl.BlockSpec((None, tc, d), lambda i, j: (i, j, 0)),
        out_shape=jax.ShapeDtypeStruct((e, cap, d), F32),
        compiler_params=_params("arbitrary", "arbitrary"),
        name="expert_ffn",
    )(xe, gate, w_up, w_down)


COMBINE_TILE = 128
COMBINE_UNROLL = 4


def _combine_window_start(first_row, total_rows):
    return pl.multiple_of(jnp.minimum((first_row // COMBINE_TILE) * COMBINE_TILE, total_rows - 2 * COMBINE_TILE),
                          COMBINE_TILE)


def _combine_kernel(total_rows, tok_ref, r0_ref, x_ref, *refs):
    ye_refs, o_ref = refs[:N_EXPERTS], refs[N_EXPERTS]
    t = pl.program_id(0)
    tok0 = t * COMBINE_TILE
    o_ref[...] = x_ref[...]

    for e in range(N_EXPERTS):
        lo = r0_ref[t * N_EXPERTS + e]
        hi = r0_ref[(t + 1) * N_EXPERTS + e]
        buf_ref = ye_refs[e]
        row0 = _combine_window_start(lo, total_rows)

        def body_n(i, carry):
            r = lo + COMBINE_UNROLL * i
            toks = [tok_ref[r + j] - tok0 for j in range(COMBINE_UNROLL)]
            vals = [o_ref[pl.ds(toks[j], 1), :] + buf_ref[pl.ds(r + j - row0, 1), :] for j in range(COMBINE_UNROLL)]
            for j in range(COMBINE_UNROLL):
                o_ref[pl.ds(toks[j], 1), :] = vals[j]
            return carry

        n_full = (hi - lo) // COMBINE_UNROLL
        lax.fori_loop(0, n_full, body_n, 0)

        def body_1(r, carry):
            tk = tok_ref[r] - tok0
            o_ref[pl.ds(tk, 1), :] = o_ref[pl.ds(tk, 1), :] + buf_ref[pl.ds(r - row0, 1), :]
            return carry

        lax.fori_loop(lo + COMBINE_UNROLL * n_full, hi, body_1, 0)


def _combine(x2, ye, idx):
    n, d = x2.shape
    n_exp, cap = idx.shape
    nt = n // COMBINE_TILE
    total_rows = n_exp * cap
    bounds = jnp.arange(nt + 1, dtype=jnp.int32) * COMBINE_TILE
    r0 = jnp.sum((idx[:, :, None] < bounds[None, None, :]).astype(jnp.int32), axis=1)
    r0 = (r0 + (jnp.arange(n_exp, dtype=jnp.int32) * cap)[:, None]).T.reshape(-1)

    def ye_spec(e):
        return pl.BlockSpec((pl.Element(2 * COMBINE_TILE), pl.Element(d)),
                            lambda t, tok, r0s: (_combine_window_start(r0s[t * N_EXPERTS + e], total_rows), 0))

    grid_spec = pltpu.PrefetchScalarGridSpec(
        num_scalar_prefetch=2,
        grid=(nt,),
        in_specs=[pl.BlockSpec((COMBINE_TILE, d), lambda t, tok, r0s: (t, 0))] + [ye_spec(e) for e in range(n_exp)],
        out_specs=pl.BlockSpec((COMBINE_TILE, d), lambda t, tok, r0s: (t, 0)),
    )
    return pl.pallas_call(
        functools.partial(_combine_kernel, total_rows),
        grid_spec=grid_spec,
        out_shape=jax.ShapeDtypeStruct((n, d), x2.dtype),
        input_output_aliases={2: 0},
        compiler_params=_params("arbitrary"),
        name="expert_combine",
    )(idx.reshape(-1).astype(jnp.int32), r0, x2, *([ye] * n_exp))


def _norm_kernel(x_ref, w_ref, o_ref):
    x = x_ref[...]
    ms = jnp.mean(x * x, axis=-1, keepdims=True)
    o_ref[...] = x * lax.rsqrt(ms + EPS) * w_ref[...]


def _final_norm(x2, w):
    n, d = x2.shape
    tm = min(1024, n)
    return pl.pallas_call(
        _norm_kernel,
        grid=(n // tm,),
        in_specs=[pl.BlockSpec((tm, d), lambda i: (i, 0)), pl.BlockSpec((1, d), lambda i: (0, 0))],
        out_specs=pl.BlockSpec((tm, d), lambda i: (i, 0)),
        out_shape=jax.ShapeDtypeStruct((n, d), F32),
        compiler_params=_params("arbitrary"),
        name="final_norm",
    )(x2, w)


def _pad_lanes(v):
    return jnp.pad(v, (0, LANES - v.shape[0]))[None, :]


def _prepare(norm_mix, w_in, conv_w, conv_b, ssd_dt_bias, ssd_a_log, ssd_d, ssd_norm, ret_log2_decay,
             attn_q_norm, attn_k_norm, w_branch, w_out, norm_ffn, w_router, w_up, w_down, norm_final):
    p = dict(
        norm_mix=norm_mix[:, None, :],
        w_proj=_repack_proj_weights(w_in),
        conv_w=conv_w, conv_b=conv_b[:, None, :],
        dtb=[_pad_lanes(ssd_dt_bias[l].reshape(-1)) for l in range(DEPTH)],
        alog=[_pad_lanes(ssd_a_log[l].reshape(-1)) for l in range(DEPTH)],
        dsk=[jnp.repeat(ssd_d[l], SSD_HEAD_DIM)[None, :] for l in range(DEPTH)],
        ssd_norm=ssd_norm[:, None, :],
        log_gamma=jnp.log1p(-jnp.exp2(ret_log2_decay.astype(F32))).reshape(DEPTH, 2 * RET_HEADS),
        q_norm=attn_q_norm, k_norm=attn_k_norm,
        w_branch=w_branch.astype(BF16), w_out=w_out.astype(BF16),
        norm_ffn=norm_ffn[:, None, :],
        w_router=jnp.pad(w_router, ((0, 0), (0, 0), (0, LANES - N_EXPERTS))).astype(BF16),
        w_up=w_up.astype(BF16), w_down=w_down.astype(BF16),
        norm_final=norm_final[None, :],
    )
    qw = ATT_HEADS * ATT_HEAD_DIM
    hid = np.arange(qw) // ATT_HEAD_DIM
    p["ones_bd"] = jnp.asarray(hid[:, None] == hid[None, :], dtype=BF16)
    return p


def _trunk(x, p):
    bsz, seq, d = x.shape
    n = bsz * seq
    cap = (EC_CAPACITY_FACTOR * n) // N_EXPERTS
    ret_cos, ret_sin = _rotary_tables(seq, RET_DIM)
    att_cos, att_sin = [jnp.tile(t, (1, ATT_KV_HEADS)) for t in _rotary_tables(seq, ATT_HEAD_DIM)]
    fft_tables = _fft_tables(seq)
    x2 = x.reshape(n, d)
    for l in range(DEPTH):
        proj = _norm_proj(x2, p["norm_mix"], p["w_proj"], l)
        o_ssd = _ssd_mixer(proj, bsz, seq, p["conv_w"][l], p["conv_b"][l], p["dtb"][l], p["alog"][l], p["dsk"][l],
                           p["ssd_norm"][l])
        o_ret = _ret_mixer(proj, bsz, seq, p["log_gamma"][l], ret_cos, ret_sin)
        o_att = _att_mixer(proj, bsz, seq, p["q_norm"][l], p["k_norm"][l], att_cos, att_sin, p["ones_bd"])
        o_fft = _fft_mixer(proj, bsz, seq, fft_tables)
        x2, u, aff = _merge(x2, proj, (o_ssd, o_ret, o_att, o_fft), p["w_branch"], p["w_out"], p["norm_ffn"],
                            p["w_router"], l)
        gate, idx = lax.top_k(aff[:, :N_EXPERTS].T, cap)
        idx, gate = lax.sort_key_val(idx, gate, dimension=1)
        u_src = jnp.concatenate([u, u], axis=0) if n < 32768 else u
        xe = u_src.at[idx.reshape(-1)].get(mode="promise_in_bounds").reshape(N_EXPERTS, cap, d)
        ye = _expert_ffn(xe, gate[..., None], p["w_up"], p["w_down"], l)
        x2 = _combine(x2, ye.reshape(-1, d), idx)
    return _final_norm(x2, p["norm_final"]).reshape(bsz, seq, d)


def kernel(x_prompt, x_sample, norm_mix, w_in, conv_w, conv_b, ssd_dt_bias, ssd_a_log, ssd_d, ssd_norm, ret_log2_decay, attn_q_norm, attn_k_norm, w_branch, w_out, norm_ffn, w_router, w_up, w_down, norm_final):
    p = _prepare(norm_mix, w_in, conv_w, conv_b, ssd_dt_bias, ssd_a_log, ssd_d, ssd_norm, ret_log2_decay,
                 attn_q_norm, attn_k_norm, w_branch, w_out, norm_ffn, w_router, w_up, w_down, norm_final)
    return (_trunk(x_prompt, p), _trunk(x_sample, p))
```

```python
import functools
import math

import numpy as np
import jax
import jax.numpy as jnp
from jax import lax
from jax.experimental import pallas as pl
from jax.experimental.pallas import tpu as pltpu

F32 = jnp.float32
BF16 = jnp.bfloat16

D_MODEL = 1024
DEPTH = 4
GRID_W = 64
CHUNK = 128
EPS = 1e-6
ROPE_THETA = 10000.0

SSD_HEADS = 8
SSD_HEAD_DIM = 64
SSD_WIDTH = 512
SSD_GROUPS = 2
SSD_STATE = 128
SSD_CONV = 5
SSD_XBC = 1024

RET_HEADS = 4
RET_DIM = 128

ATT_HEADS = 8
ATT_KV_HEADS = 2
ATT_HEAD_DIM = 64

FNET_GROUPS = 4
FNET_GROUP_DIM = 128

N_BRANCHES = 4
N_EXPERTS = 16
EXPERT_FF = 1024
EC_CAPACITY_FACTOR = 2

IN_SIZES = (512, 1024, 8, 8, 512, 512, 512, 512, 512, 128, 128, 512, 4096)
D_IN = sum(IN_SIZES)

LANES = 128
CONV_HALO = 16

COL_XBC, COL_Z, COL_RQ, COL_RK, COL_RV, COL_RG, COL_AQ = 0, 8, 12, 16, 20, 24, 28
COL_GATES, COL_F, COL_AK, COL_AV, COL_DT = 32, 64, 68, 69, 70
PROJ_COLS = 72 * LANES


def _repack_proj_weights(w_in):
    offs = np.concatenate([[0], np.cumsum(IN_SIZES)])
    (z, xbc, dtf, dtb, rq, rk, rv, rg, aq, ak, av, f, gates) = [w_in[..., offs[i]:offs[i + 1]] for i in range(13)]
    pad = jnp.zeros(w_in.shape[:-1] + (2 * LANES - 16,), w_in.dtype)
    w = jnp.concatenate([xbc, z, rq, rk, rv, rg, aq, gates, f, ak, av, dtf, dtb, pad], axis=-1)
    assert w.shape[-1] == PROJ_COLS
    return w.astype(BF16)


def _params(*sem):
    return pltpu.CompilerParams(dimension_semantics=sem, vmem_limit_bytes=56 * 1024 * 1024)


def _silu(x):
    return x * jax.nn.sigmoid(x)


def _proj_kernel(x_ref, nw_ref, w_ref, o_ref, u_scr):
    @pl.when(pl.program_id(1) == 0)
    def _():
        x = x_ref[...]
        ms = jnp.mean(x * x, axis=-1, keepdims=True)
        u_scr[...] = (x * lax.rsqrt(ms + EPS) * nw_ref[...]).astype(BF16)

    o_ref[...] = jnp.dot(u_scr[...], w_ref[...], preferred_element_type=F32).astype(o_ref.dtype)


def _norm_proj(x2, nw, w_all, layer):
    n, d = x2.shape
    cols = w_all.shape[-1]
    tm = min(1024, n)
    tn = 3072
    return pl.pallas_call(
        _proj_kernel,
        grid=(n // tm, cols // tn),
        in_specs=[pl.BlockSpec((tm, d), lambda i, j: (i, 0)),
                  pl.BlockSpec((None, 1, d), lambda i, j: (layer, 0, 0)),
                  pl.BlockSpec((None, d, tn), lambda i, j: (layer, 0, j))],
        out_specs=pl.BlockSpec((tm, tn), lambda i, j: (i, j)),
        out_shape=jax.ShapeDtypeStruct((n, cols), BF16),
        scratch_shapes=[pltpu.VMEM((tm, d), BF16)],
        compiler_params=_params("arbitrary", "arbitrary"),
        name="norm_proj",
    )(x2, nw, w_all)


def _cumsum_rows(x, reverse):
    n = x.shape[0]
    row = lax.broadcasted_iota(jnp.int32, x.shape, 0)
    d = 1
    while d < n:
        if reverse:
            x = x + jnp.where(row < n - d, pltpu.roll(x, n - d, axis=0), 0.0)
        else:
            x = x + jnp.where(row >= d, pltpu.roll(x, d, axis=0), 0.0)
        d *= 2
    return x


def _softplus(x):
    return jnp.maximum(x, 0.0) + jnp.log1p(jnp.exp(-jnp.abs(x)))


def _ssd_kernel(rev, *refs):
    if rev:
        (act_ref, dt_ref, dtb_ref, alog_ref, z_ref, yf_ref, nw_ref, o_ref, h_scr) = refs
    else:
        (prev_ref, main_ref, next_ref, dt_ref, cw_ref, cb_ref, dtb_ref, alog_ref, dsk_ref,
         o_ref, act_ref, ext_scr, h_scr) = refs
    c = pl.program_id(1)
    nc = pl.num_programs(1)
    cc = nc - 1 - c if rev else c

    @pl.when(c == 0)
    def _():
        h_scr[...] = jnp.zeros_like(h_scr)

    if rev:
        act = act_ref[...].astype(F32)
    else:
        ext_scr[0:CONV_HALO, :] = jnp.where(cc == 0, 0.0, prev_ref[...].astype(F32))
        ext_scr[CONV_HALO:CONV_HALO + CHUNK, :] = main_ref[...].astype(F32)
        ext_scr[CONV_HALO + CHUNK:, :] = jnp.where(cc == nc - 1, 0.0, next_ref[...].astype(F32))
        acc = jnp.broadcast_to(cb_ref[...], (CHUNK, SSD_XBC))
        for k in range(SSD_CONV):
            acc = acc + cw_ref[k:k + 1, :] * ext_scr[pl.ds(CONV_HALO - SSD_CONV // 2 + k, CHUNK), :]
        act = _silu(acc)
        act_ref[...] = act.astype(act_ref.dtype)
    gn = SSD_GROUPS * SSD_STATE
    x = act[:, :SSD_WIDTH]
    b_mat = act[:, SSD_WIDTH:SSD_WIDTH + gn]
    c_mat = act[:, SSD_WIDTH + gn:]

    off = SSD_HEADS if rev else 0
    dt = _softplus(dt_ref[...].astype(F32) + dtb_ref[...])
    a = -jnp.exp(alog_ref[...])
    cs = _cumsum_rows(dt * a, rev)
    cs_t = cs.T
    rowi = lax.broadcasted_iota(jnp.int32, (CHUNK, CHUNK), 0)
    coli = lax.broadcasted_iota(jnp.int32, (CHUNK, CHUNK), 1)
    mask = (rowi < coli) if rev else (rowi >= coli)

    def expand(v):
        return jnp.concatenate(
            [jnp.broadcast_to(v[:, off + h:off + h + 1], (CHUNK, SSD_HEAD_DIM)) for h in range(SSD_HEADS)], axis=1)

    dt_e = expand(dt)
    cs_e = expand(cs)
    edge = cs_e[0:1, :] if rev else cs_e[CHUNK - 1:CHUNK, :]
    xdt = x * dt_e
    xst = (xdt * jnp.exp(edge - cs_e)).astype(BF16)
    xdt = xdt.astype(BF16)
    e_in = jnp.exp(cs_e)
    e_chunk = jnp.exp(edge)

    r = SSD_HEADS // SSD_GROUPS
    gw = r * SSD_HEAD_DIM
    ys = []
    for g in range(SSD_GROUPS):
        bg = b_mat[:, g * SSD_STATE:(g + 1) * SSD_STATE].astype(BF16)
        cg = c_mat[:, g * SSD_STATE:(g + 1) * SSD_STATE].astype(BF16)
        scores = lax.dot_general(cg, bg, (((1,), (1,)), ((), ())), preferred_element_type=F32)
        h_prev = h_scr[g]
        y_off = jnp.dot(cg, h_prev.astype(BF16), preferred_element_type=F32) * e_in[:, g * gw:(g + 1) * gw]
        st = lax.dot_general(bg, xst[:, g * gw:(g + 1) * gw], (((0,), (0,)), ((), ())),
                             preferred_element_type=F32)
        h_scr[g] = h_prev * e_chunk[:, g * gw:(g + 1) * gw] + st
        for j in range(r):
            h = g * r + j
            colb = jnp.broadcast_to(cs[:, off + h:off + h + 1], (CHUNK, CHUNK))
            rowb = cs_t[off + h:off + h + 1, :]
            decay = jnp.exp(jnp.where(mask, colb - rowb, -jnp.inf))
            m = (scores * decay).astype(BF16)
            y_d = jnp.dot(m, xdt[:, h * SSD_HEAD_DIM:(h + 1) * SSD_HEAD_DIM], preferred_element_type=F32)
            ys.append(y_d + y_off[:, j * SSD_HEAD_DIM:(j + 1) * SSD_HEAD_DIM])
    y = jnp.concatenate(ys, axis=1)

    if rev:
        y = (y + yf_ref[...]) * _silu(z_ref[...].astype(F32))
        ms = jnp.mean(y * y, axis=-1, keepdims=True)
        o_ref[...] = (y * lax.rsqrt(ms + EPS) * nw_ref[...]).astype(o_ref.dtype)
    else:
        o_ref[...] = y + x * dsk_ref[...]


def _ssd_mixer(proj, bsz, seq, conv_w, conv_b, dtb_row, alog_row, dsk_row, norm_w):
    n = bsz * seq
    nc = seq // CHUNK
    hb = CHUNK // CONV_HALO
    nhalo = n // CONV_HALO

    state = pltpu.VMEM((SSD_GROUPS, SSD_STATE, SSD_WIDTH // SSD_GROUPS), F32)
    const = lambda shape: pl.BlockSpec(shape, lambda b, c: (0, 0))
    row = lambda b, c: b * nc + c
    y_f, act = pl.pallas_call(
        functools.partial(_ssd_kernel, False),
        grid=(bsz, nc),
        in_specs=[
            pl.BlockSpec((CONV_HALO, SSD_XBC), lambda b, c: (jnp.maximum(row(b, c) * hb - 1, 0), COL_XBC // 8)),
            pl.BlockSpec((CHUNK, SSD_XBC), lambda b, c: (row(b, c), COL_XBC // 8)),
            pl.BlockSpec((CONV_HALO, SSD_XBC), lambda b, c: (jnp.minimum((row(b, c) + 1) * hb, nhalo - 1), COL_XBC // 8)),
            pl.BlockSpec((CHUNK, LANES), lambda b, c: (row(b, c), COL_DT)),
            const((SSD_CONV, SSD_XBC)), const((1, SSD_XBC)), const((1, LANES)), const((1, LANES)), const((1, SSD_WIDTH)),
        ],
        out_specs=[pl.BlockSpec((CHUNK, SSD_WIDTH), lambda b, c: (row(b, c), 0)),
                   pl.BlockSpec((CHUNK, SSD_XBC), lambda b, c: (row(b, c), 0))],
        out_shape=[jax.ShapeDtypeStruct((n, SSD_WIDTH), F32), jax.ShapeDtypeStruct((n, SSD_XBC), BF16)],
        scratch_shapes=[pltpu.VMEM((CHUNK + 2 * CONV_HALO, SSD_XBC), F32), state],
        compiler_params=_params("arbitrary", "arbitrary"),
        name="ssd_fwd",
    )(proj, proj, proj, proj, conv_w, conv_b, dtb_row, alog_row, dsk_row)

    rrow = lambda b, c: b * nc + (nc - 1 - c)
    return pl.pallas_call(
        functools.partial(_ssd_kernel, True),
        grid=(bsz, nc),
        in_specs=[
            pl.BlockSpec((CHUNK, SSD_XBC), lambda b, c: (rrow(b, c), 0)),
            pl.BlockSpec((CHUNK, LANES), lambda b, c: (rrow(b, c), COL_DT)),
            const((1, LANES)), const((1, LANES)),
            pl.BlockSpec((CHUNK, SSD_WIDTH), lambda b, c: (rrow(b, c), COL_Z // 4)),
            pl.BlockSpec((CHUNK, SSD_WIDTH), lambda b, c: (rrow(b, c), 0)),
            const((1, SSD_WIDTH)),
        ],
        out_specs=pl.BlockSpec((CHUNK, SSD_WIDTH), lambda b, c: (rrow(b, c), 0)),
        out_shape=jax.ShapeDtypeStruct((n, SSD_WIDTH), BF16),
        scratch_shapes=[state],
        compiler_params=_params("arbitrary", "arbitrary"),
        name="ssd_bwd",
    )(act, proj, dtb_row, alog_row, proj, y_f, norm_w)


def _rotary_tables(seq, head_dim):
    n_freq = head_dim // 4
    freqs = ROPE_THETA ** (-jnp.arange(n_freq, dtype=F32) / n_freq)
    t = jnp.arange(seq)
    ang_row = (t // GRID_W).astype(F32)[:, None] * freqs
    ang_col = (t % GRID_W).astype(F32)[:, None] * freqs
    cos = jnp.concatenate([jnp.cos(ang_row)] * 2 + [jnp.cos(ang_col)] * 2, axis=1)
    sin = jnp.concatenate([-jnp.sin(ang_row), jnp.sin(ang_row), -jnp.sin(ang_col), jnp.sin(ang_col)], axis=1)
    return cos, sin


def _rotate(x, cos, sin_signed, quarter):
    w = x.shape[1]
    lane = lax.broadcasted_iota(jnp.int32, x.shape, 1)
    partner = jnp.where((lane % (2 * quarter)) < quarter,
                        pltpu.roll(x, w - quarter, axis=1), pltpu.roll(x, quarter, axis=1))
    return x * cos + partner * sin_signed


def _ret_kernel(rev, *refs):
    if rev:
        (lg_ref, q_ref, k_ref, v_ref, cos_ref, sin_ref, g_ref, yf_ref, o_ref, tab_scr, h_scr) = refs
    else:
        (lg_ref, q_ref, k_ref, v_ref, cos_ref, sin_ref, o_ref, tab_scr, h_scr) = refs
    first = jnp.logical_and(pl.program_id(0) == 0, pl.program_id(1) == 0)

    @pl.when(first)
    def _():
        rowi = lax.broadcasted_iota(jnp.int32, (CHUNK, CHUNK), 0)
        coli = lax.broadcasted_iota(jnp.int32, (CHUNK, CHUNK), 1)
        rowf = rowi.astype(F32)
        for h in range(RET_HEADS):
            lg = lg_ref[(RET_HEADS if rev else 0) + h]
            if rev:
                dist = jnp.where(coli > rowi, (coli - rowi).astype(F32) * lg, -jnp.inf)
                kd = rowf * lg
                qd = (CHUNK - rowf) * lg
            else:
                dist = jnp.where(rowi >= coli, (rowi - coli).astype(F32) * lg, -jnp.inf)
                kd = (CHUNK - 1 - rowf) * lg
                qd = (rowf + 1.0) * lg
            tab_scr[0, h] = jnp.exp(dist)
            tab_scr[1, h] = jnp.exp(kd)
            tab_scr[2, h] = jnp.exp(qd)
            tab_scr[3, h] = jnp.exp(jnp.full((CHUNK, CHUNK), CHUNK, F32) * lg)

    @pl.when(pl.program_id(1) == 0)
    def _():
        h_scr[...] = jnp.zeros_like(h_scr)

    cos = jnp.concatenate([cos_ref[...]] * RET_HEADS, axis=1)
    sin = jnp.concatenate([sin_ref[...]] * RET_HEADS, axis=1)
    q = _rotate(q_ref[...].astype(F32), cos, sin, RET_DIM // 4)
    k = _rotate(k_ref[...].astype(F32), cos, sin, RET_DIM // 4) * (RET_DIM ** -0.5)
    v = v_ref[...].astype(F32)
    ys = []
    for h in range(RET_HEADS):
        sl = slice(h * RET_DIM, (h + 1) * RET_DIM)
        qh = q[:, sl].astype(BF16)
        kh = k[:, sl].astype(BF16)
        vh = v[:, sl]
        s = lax.dot_general(qh, kh, (((1,), (1,)), ((), ())), preferred_element_type=F32) * tab_scr[0, h]
        inner = jnp.dot(s.astype(BF16), vh.astype(BF16), preferred_element_type=F32)
        h_prev = h_scr[h]
        cross = jnp.dot(qh, h_prev.astype(BF16), preferred_element_type=F32) * tab_scr[2, h]
        st = lax.dot_general(kh, (vh * tab_scr[1, h]).astype(BF16), (((0,), (0,)), ((), ())),
                             preferred_element_type=F32)
        h_scr[h] = h_prev * tab_scr[3, h] + st
        ys.append(inner + cross)

    if rev:
        outs = []
        for h in range(RET_HEADS):
            sl = slice(h * RET_DIM, (h + 1) * RET_DIM)
            y = ys[h] + yf_ref[:, sl]
            y = y * lax.rsqrt(jnp.mean(y * y, axis=-1, keepdims=True) + EPS)
            outs.append(y * _silu(g_ref[:, sl].astype(F32)))
        o_ref[...] = jnp.concatenate(outs, axis=1).astype(o_ref.dtype)
    else:
        o_ref[...] = jnp.concatenate(ys, axis=1)


def _ret_mixer(proj, bsz, seq, log_gamma, cos, sin):
    n = bsz * seq
    nc = seq // CHUNK
    width = RET_HEADS * RET_DIM

    def call(rev, extra_in, extra_cols, out_dtype):
        chunk = (lambda c: nc - 1 - c) if rev else (lambda c: c)
        row = lambda b, c: b * nc + chunk(c)
        pcol = lambda cb: pl.BlockSpec((CHUNK, width), lambda b, c: (row(b, c), cb))
        tab = pl.BlockSpec((CHUNK, RET_DIM), lambda b, c: (chunk(c), 0))
        in_specs = [pl.BlockSpec(memory_space=pltpu.SMEM),
                    pcol(COL_RQ // 4), pcol(COL_RK // 4), pcol(COL_RV // 4), tab, tab] + [pcol(cb) for cb in extra_cols]
        return pl.pallas_call(
            functools.partial(_ret_kernel, rev),
            grid=(bsz, nc),
            in_specs=in_specs,
            out_specs=pl.BlockSpec((CHUNK, width), lambda b, c: (row(b, c), 0)),
            out_shape=jax.ShapeDtypeStruct((n, width), out_dtype),
            scratch_shapes=[pltpu.VMEM((4, RET_HEADS, CHUNK, CHUNK), F32),
                            pltpu.VMEM((RET_HEADS, RET_DIM, RET_DIM), F32)],
            compiler_params=_params("arbitrary", "arbitrary"),
            name="ret_bwd" if rev else "ret_fwd",
        )(log_gamma, proj, proj, proj, cos, sin, *extra_in)

    y_f = call(False, (), (), F32)
    return call(True, (proj, y_f), (COL_RG // 4, 0), BF16)


def _head_mean_square(x, ones_bd):
    x2 = x * x
    hi = x2.astype(BF16)
    lo = (x2 - hi.astype(F32)).astype(BF16)
    tot = jnp.dot(hi, ones_bd, preferred_element_type=F32) + jnp.dot(lo, ones_bd, preferred_element_type=F32)
    return tot * (1.0 / ATT_HEAD_DIM)


def _pad_heads(x, n_heads, fill=None):
    pad = jnp.zeros((x.shape[0], LANES - ATT_HEAD_DIM), x.dtype) if fill is None else fill
    parts = []
    for h in range(n_heads):
        parts += [x[:, h * ATT_HEAD_DIM:(h + 1) * ATT_HEAD_DIM], pad]
    return jnp.concatenate(parts, axis=1)


def _att_prep_kernel(q_ref, k_ref, v_ref, qn_ref, kn_ref, cos_ref, sin_ref, ones_ref, qo_ref, ko_ref, vo_ref):
    rep = ATT_HEADS // ATT_KV_HEADS
    cos1 = cos_ref[...]
    sin1 = sin_ref[...]
    q = q_ref[...].astype(F32)
    q = q * lax.rsqrt(_head_mean_square(q, ones_ref[...]) + EPS) * qn_ref[...]
    q = _rotate(q, jnp.concatenate([cos1] * rep, axis=1), jnp.concatenate([sin1] * rep, axis=1), ATT_HEAD_DIM // 4)
    qo_ref[...] = _pad_heads((q * (ATT_HEAD_DIM ** -0.5)).astype(qo_ref.dtype), ATT_HEADS)
    k = k_ref[...].astype(F32)
    kw = ATT_KV_HEADS * ATT_HEAD_DIM
    k = k * lax.rsqrt(_head_mean_square(k, ones_ref[0:kw, 0:kw]) + EPS) * kn_ref[...]
    k = _rotate(k, cos1, sin1, ATT_HEAD_DIM // 4)
    ko_ref[...] = _pad_heads(k.astype(ko_ref.dtype), ATT_KV_HEADS)
    v = v_ref[...]
    vo_ref[...] = _pad_heads(v, ATT_KV_HEADS, fill=jnp.ones((v.shape[0], LANES - ATT_HEAD_DIM), v.dtype))


def _flash_kernel(tq, tk, q_ref, k_ref, v_ref, o_ref, m_scr, acc_scr):
    ki = pl.program_id(2)
    rep = ATT_HEADS // ATT_KV_HEADS

    @pl.when(ki == 0)
    def _():
        m_scr[...] = jnp.full_like(m_scr, -jnp.inf)
        acc_scr[...] = jnp.zeros_like(acc_scr)

    for g in range(ATT_KV_HEADS):
        qs = jnp.concatenate([q_ref[:, (rep * g + j) * LANES:(rep * g + j + 1) * LANES] for j in range(rep)], axis=0)
        kg = k_ref[:, g * LANES:(g + 1) * LANES]
        vg = v_ref[:, g * LANES:(g + 1) * LANES]
        s = lax.dot_general(qs, kg, (((1,), (1,)), ((), ())), preferred_element_type=F32)
        m_prev = m_scr[g]
        m_new = jnp.maximum(m_prev, jnp.max(s, axis=1, keepdims=True))
        p = jnp.exp(s - jnp.tile(m_new, (1, tk // LANES)))
        alpha = jnp.exp(m_prev - m_new)
        acc_scr[g] = alpha * acc_scr[g] + jnp.dot(p.astype(BF16), vg, preferred_element_type=F32)
        m_scr[g] = m_new

    @pl.when(ki == pl.num_programs(2) - 1)
    def _():
        outs = []
        for g in range(ATT_KV_HEADS):
            for j in range(rep):
                rows = acc_scr[g, j * tq:(j + 1) * tq, :]
                outs.append(rows[:, :ATT_HEAD_DIM] / rows[:, ATT_HEAD_DIM:])
        o_ref[...] = jnp.concatenate(outs, axis=1).astype(o_ref.dtype)


def _att_mixer(proj, bsz, seq, q_norm, k_norm, cos, sin, ones_bd):
    n = bsz * seq
    qw = ATT_HEADS * ATT_HEAD_DIM
    kw = ATT_KV_HEADS * ATT_HEAD_DIM
    qpw = ATT_HEADS * LANES
    kpw = ATT_KV_HEADS * LANES
    tm = min(512, seq)
    nt = seq // tm
    qn_row = jnp.tile(q_norm, ATT_HEADS)[None, :]
    kn_row = jnp.tile(k_norm, ATT_KV_HEADS)[None, :]
    q_pad, k_pad, v_ext = pl.pallas_call(
        _att_prep_kernel,
        grid=(bsz, nt),
        in_specs=[pl.BlockSpec((tm, qw), lambda b, i: (b * nt + i, COL_AQ // 4)),
                  pl.BlockSpec((tm, kw), lambda b, i: (b * nt + i, COL_AK)),
                  pl.BlockSpec((tm, kw), lambda b, i: (b * nt + i, COL_AV)),
                  pl.BlockSpec((1, qw), lambda b, i: (0, 0)),
                  pl.BlockSpec((1, kw), lambda b, i: (0, 0)),
                  pl.BlockSpec((tm, kw), lambda b, i: (i, 0)),
                  pl.BlockSpec((tm, kw), lambda b, i: (i, 0)),
                  pl.BlockSpec((qw, qw), lambda b, i: (0, 0))],
        out_specs=[pl.BlockSpec((tm, qpw), lambda b, i: (b * nt + i, 0)),
                   pl.BlockSpec((tm, kpw), lambda b, i: (b * nt + i, 0)),
                   pl.BlockSpec((tm, kpw), lambda b, i: (b * nt + i, 0))],
        out_shape=[jax.ShapeDtypeStruct((n, qpw), BF16), jax.ShapeDtypeStruct((n, kpw), BF16),
                   jax.ShapeDtypeStruct((n, kpw), BF16)],
        compiler_params=_params("arbitrary", "arbitrary"),
        name="att_prep",
    )(proj, proj, proj, qn_row, kn_row, cos, sin, ones_bd)

    tq = min(512, seq)
    tk = min(2048, seq)
    nq = seq // tq
    nk = seq // tk
    rep = ATT_HEADS // ATT_KV_HEADS
    return pl.pallas_call(
        functools.partial(_flash_kernel, tq, tk),
        grid=(bsz, nq, nk),
        in_specs=[pl.BlockSpec((tq, qpw), lambda b, i, j: (b * nq + i, 0)),
                  pl.BlockSpec((tk, kpw), lambda b, i, j: (b * nk + j, 0)),
                  pl.BlockSpec((tk, kpw), lambda b, i, j: (b * nk + j, 0))],
        out_specs=pl.BlockSpec((tq, qw), lambda b, i, j: (b * nq + i, 0)),
        out_shape=jax.ShapeDtypeStruct((n, qw), BF16),
        scratch_shapes=[pltpu.VMEM((ATT_KV_HEADS, rep * tq, LANES), F32),
                        pltpu.VMEM((ATT_KV_HEADS, rep * tq, LANES), F32)],
        compiler_params=_params("arbitrary", "arbitrary", "arbitrary"),
        name="flash_attention",
    )(q_pad, k_pad, v_ext)


def _fft_split(seq):
    l1 = 1 << (int(math.log2(seq)) // 2)
    return l1, seq // l1


def _fft_tables(seq):
    l1, l2 = _fft_split(seq)
    c = np.arange(FNET_GROUP_DIM)
    ang = 2.0 * np.pi * ((c[:, None] * c[None, :]) % FNET_GROUP_DIM) / FNET_GROUP_DIM
    chan = np.concatenate([np.cos(ang), -np.sin(ang)], axis=1)
    k1 = np.arange(l1)
    ang1 = 2.0 * np.pi * ((k1[:, None] * k1[None, :]) % l1) / l1
    t2 = np.arange(l2)
    k2 = np.arange(l2)
    kk = k1[:, None, None] + l1 * k2[None, :, None]
    ang2 = 2.0 * np.pi * ((kk * t2[None, None, :]) % seq) / seq
    stage2 = np.concatenate([np.cos(ang2), np.sin(ang2)], axis=2)
    to = lambda a: jnp.asarray(a, dtype=BF16)
    stage1 = np.block([[np.cos(ang1), np.sin(ang1)], [-np.sin(ang1), np.cos(ang1)]])
    return to(chan), to(stage1), to(stage2)


FFT_ROWS = 8


def _fft_chan_kernel(f_ref, chan_ref, zr_ref, zi_ref):
    zr, zi = [], []
    for g in range(FNET_GROUPS):
        z = jnp.dot(f_ref[:, g * FNET_GROUP_DIM:(g + 1) * FNET_GROUP_DIM], chan_ref[...],
                    preferred_element_type=F32)
        zr.append(z[:, :FNET_GROUP_DIM])
        zi.append(z[:, FNET_GROUP_DIM:])
    zr_ref[...] = jnp.concatenate(zr, axis=1)
    zi_ref[...] = jnp.concatenate(zi, axis=1)


def _fft_a_kernel(zr_ref, zi_ref, w1_ref, ur_ref, ui_ref):
    w1 = w1_ref[...]
    l1 = w1.shape[0] // 2
    for j in range(FFT_ROWS):
        z = jnp.concatenate([zr_ref[:, j, :], zi_ref[:, j, :]], axis=0).astype(BF16)
        u = jnp.dot(w1, z, preferred_element_type=F32)
        ur_ref[:, j, :] = u[:l1]
        ui_ref[:, j, :] = u[l1:]


def _fft_b_kernel(scale, ur_ref, ui_ref, m_ref, o_ref):
    for j in range(FFT_ROWS):
        u = jnp.concatenate([ur_ref[j], ui_ref[j]], axis=0).astype(BF16)
        o_ref[:, j, :] = jnp.dot(m_ref[j], u, preferred_element_type=F32) * scale


def _fft_mixer(proj, bsz, seq, tables):
    chan, stage1, stage2 = tables
    l1, l2 = _fft_split(seq)
    n = bsz * seq
    w = FNET_GROUPS * FNET_GROUP_DIM
    tm = min(1024, n)
    z_r, z_i = pl.pallas_call(
        _fft_chan_kernel,
        grid=(n // tm,),
        in_specs=[pl.BlockSpec((tm, w), lambda i: (i, COL_F // 4)),
                  pl.BlockSpec((FNET_GROUP_DIM, 2 * FNET_GROUP_DIM), lambda i: (0, 0))],
        out_specs=[pl.BlockSpec((tm, w), lambda i: (i, 0))] * 2,
        out_shape=[jax.ShapeDtypeStruct((n, w), F32)] * 2,
        compiler_params=_params("arbitrary"),
        name="fft_channels",
    )(proj, chan)
    slab = pl.BlockSpec((None, l1, FFT_ROWS, w), lambda b, t: (b, 0, t, 0))
    u_r, u_i = pl.pallas_call(
        _fft_a_kernel,
        grid=(bsz, l2 // FFT_ROWS),
        in_specs=[slab, slab, pl.BlockSpec((2 * l1, 2 * l1), lambda b, t: (0, 0))],
        out_specs=[slab, slab],
        out_shape=[jax.ShapeDtypeStruct((bsz, l1, l2, w), F32)] * 2,
        compiler_params=_params("arbitrary", "arbitrary"),
        name="fft_stage_a",
    )(z_r.reshape(bsz, l1, l2, w), z_i.reshape(bsz, l1, l2, w), stage1)
    scale = 1.0 / math.sqrt(seq * FNET_GROUP_DIM)
    rows = pl.BlockSpec((None, FFT_ROWS, l2, w), lambda b, i: (b, i, 0, 0))
    y = pl.pallas_call(
        functools.partial(_fft_b_kernel, scale),
        grid=(bsz, l1 // FFT_ROWS),
        in_specs=[rows, rows, pl.BlockSpec((FFT_ROWS, l2, 2 * l2), lambda b, i: (i, 0, 0))],
        out_specs=pl.BlockSpec((None, l2, FFT_ROWS, w), lambda b, i: (b, 0, i, 0)),
        out_shape=jax.ShapeDtypeStruct((bsz, l2, l1, w), F32),
        compiler_params=_params("arbitrary", "arbitrary"),
        name="fft_stage_b",
    )(u_r, u_i, stage2)
    return y.reshape(n, w)


def _merge_kernel(x_ref, g0_ref, g1_ref, g2_ref, g3_ref, o0_ref, o1_ref, o2_ref, o3_ref,
                  wb_ref, wo_ref, nf_ref, wr_ref, xo_ref, u_ref, aff_ref):
    merged = None
    for kbr, (g_ref, o_ref) in enumerate(((g0_ref, o0_ref), (g1_ref, o1_ref), (g2_ref, o2_ref), (g3_ref, o3_ref))):
        term = jax.nn.sigmoid(g_ref[...].astype(F32)) * jnp.dot(o_ref[...].astype(BF16), wb_ref[kbr], preferred_element_type=F32)
        merged = term if merged is None else merged + term
    x = x_ref[...] + jnp.dot(merged.astype(BF16), wo_ref[...], preferred_element_type=F32)
    xo_ref[...] = x
    ms = jnp.mean(x * x, axis=-1, keepdims=True)
    u = (x * lax.rsqrt(ms + EPS) * nf_ref[...]).astype(BF16)
    for c in range(u_ref.shape[0]):
        u_ref[c] = u
    logits = jnp.dot(u, wr_ref[...], preferred_element_type=F32)
    lane = lax.broadcasted_iota(jnp.int32, logits.shape, 1)
    logits = jnp.where(lane < N_EXPERTS, logits, -jnp.inf)
    e = jnp.exp(logits - jnp.max(logits, axis=-1, keepdims=True))
    aff_ref[...] = e / jnp.sum(e, axis=-1, keepdims=True)


def _merge(x2, proj, branches, w_branch, w_out, norm_ffn, w_router, layer, u_copies):
    n, d = x2.shape
    tm = min(256, n)
    bw = SSD_WIDTH
    gate_spec = lambda kbr: pl.BlockSpec((tm, d), lambda i: (i, COL_GATES // 8 + kbr))
    br_spec = pl.BlockSpec((tm, bw), lambda i: (i, 0))
    return pl.pallas_call(
        _merge_kernel,
        grid=(n // tm,),
        in_specs=[pl.BlockSpec((tm, d), lambda i: (i, 0))] + [gate_spec(k) for k in range(N_BRANCHES)]
                 + [br_spec] * N_BRANCHES
                 + [pl.BlockSpec((None, N_BRANCHES, bw, d), lambda i: (layer, 0, 0, 0)),
                    pl.BlockSpec((None, d, d), lambda i: (layer, 0, 0)),
                    pl.BlockSpec((None, 1, d), lambda i: (layer, 0, 0)),
                    pl.BlockSpec((None, d, LANES), lambda i: (layer, 0, 0))],
        out_specs=[pl.BlockSpec((tm, d), lambda i: (i, 0)), pl.BlockSpec((u_copies, tm, d), lambda i: (0, i, 0)),
                   pl.BlockSpec((tm, LANES), lambda i: (i, 0))],
        out_shape=[jax.ShapeDtypeStruct((n, d), F32), jax.ShapeDtypeStruct((u_copies, n, d), BF16),
                   jax.ShapeDtypeStruct((n, LANES), F32)],
        compiler_params=_params("arbitrary"),
        name="merge_out",
    )(x2, proj, proj, proj, proj, *branches, w_branch, w_out, norm_ffn, w_router)


def _ffn_kernel(xe_ref, gate_ref, wu_ref, wd_ref, o_ref):
    h = jnp.dot(xe_ref[...], wu_ref[...], preferred_element_type=F32)
    act = (_silu(h[:, :EXPERT_FF]) * h[:, EXPERT_FF:]).astype(BF16)
    o_ref[...] = jnp.dot(act, wd_ref[...], preferred_element_type=F32) * gate_ref[...]


def _expert_ffn(xe, gate, w_up, w_down, layer):
    e, cap, d = xe.shape
    tc = min(512, cap)
    return pl.pallas_call(
        _ffn_kernel,
        grid=(e, cap // tc),
        in_specs=[pl.BlockSpec((None, tc, d), lambda i, j: (i, j, 0)),
                  pl.BlockSpec((None, tc, 1), lambda i, j: (i, j, 0)),
                  pl.BlockSpec((None, None, d, 2 * EXPERT_FF), lambda i, j: (layer, i, 0, 0)),
                  pl.BlockSpec((None, None, EXPERT_FF, d), lambda i, j: (layer, i, 0, 0))],
        out_specs=pl.BlockSpec((None, tc, d), lambda i, j: (i, j, 0)),
        out_shape=jax.ShapeDtypeStruct((e, cap, d), F32),
        compiler_params=_params("arbitrary", "arbitrary"),
        name="expert_ffn",
    )(xe, gate, w_up, w_down)


COMBINE_TILE = 128
COMBINE_UNROLL = 4


def _combine_window_start(first_row, total_rows):
    return pl.multiple_of(jnp.minimum((first_row // COMBINE_TILE) * COMBINE_TILE, total_rows - 2 * COMBINE_TILE),
                          COMBINE_TILE)


def _combine_kernel(total_rows, tok_ref, r0_ref, x_ref, *refs):
    ye_refs, o_ref = refs[:N_EXPERTS], refs[N_EXPERTS]
    t = pl.program_id(0)
    tok0 = t * COMBINE_TILE
    o_ref[...] = x_ref[...]

    for e in range(N_EXPERTS):
        lo = r0_ref[t * N_EXPERTS + e]
        hi = r0_ref[(t + 1) * N_EXPERTS + e]
        buf_ref = ye_refs[e]
        row0 = _combine_window_start(lo, total_rows)

        def body_n(i, carry):
            r = lo + COMBINE_UNROLL * i
            toks = [tok_ref[r + j] - tok0 for j in range(COMBINE_UNROLL)]
            vals = [o_ref[pl.ds(toks[j], 1), :] + buf_ref[pl.ds(r + j - row0, 1), :] for j in range(COMBINE_UNROLL)]
            for j in range(COMBINE_UNROLL):
                o_ref[pl.ds(toks[j], 1), :] = vals[j]
            return carry

        n_full = (hi - lo) // COMBINE_UNROLL
        lax.fori_loop(0, n_full, body_n, 0)

        def body_1(r, carry):
            tk = tok_ref[r] - tok0
            o_ref[pl.ds(tk, 1), :] = o_ref[pl.ds(tk, 1), :] + buf_ref[pl.ds(r - row0, 1), :]
            return carry

        lax.fori_loop(lo + COMBINE_UNROLL * n_full, hi, body_1, 0)


def _combine(x2, ye, idx):
    n, d = x2.shape
    n_exp, cap = idx.shape
    nt = n // COMBINE_TILE
    total_rows = n_exp * cap
    bounds = jnp.arange(nt + 1, dtype=jnp.int32) * COMBINE_TILE
    r0 = jnp.sum((idx[:, :, None] < bounds[None, None, :]).astype(jnp.int32), axis=1)
    r0 = (r0 + (jnp.arange(n_exp, dtype=jnp.int32) * cap)[:, None]).T.reshape(-1)

    def ye_spec(e):
        return pl.BlockSpec((pl.Element(2 * COMBINE_TILE), pl.Element(d)),
                            lambda t, tok, r0s: (_combine_window_start(r0s[t * N_EXPERTS + e], total_rows), 0))

    grid_spec = pltpu.PrefetchScalarGridSpec(
        num_scalar_prefetch=2,
        grid=(nt,),
        in_specs=[pl.BlockSpec((COMBINE_TILE, d), lambda t, tok, r0s: (t, 0))] + [ye_spec(e) for e in range(n_exp)],
        out_specs=pl.BlockSpec((COMBINE_TILE, d), lambda t, tok, r0s: (t, 0)),
    )
    return pl.pallas_call(
        functools.partial(_combine_kernel, total_rows),
        grid_spec=grid_spec,
        out_shape=jax.ShapeDtypeStruct((n, d), x2.dtype),
        input_output_aliases={2: 0},
        compiler_params=_params("arbitrary"),
        name="expert_combine",
    )(idx.reshape(-1).astype(jnp.int32), r0, x2, *([ye] * n_exp))


def _norm_kernel(x_ref, w_ref, o_ref):
    x = x_ref[...]
    ms = jnp.mean(x * x, axis=-1, keepdims=True)
    o_ref[...] = x * lax.rsqrt(ms + EPS) * w_ref[...]


def _final_norm(x2, w):
    n, d = x2.shape
    tm = min(1024, n)
    return pl.pallas_call(
        _norm_kernel,
        grid=(n // tm,),
        in_specs=[pl.BlockSpec((tm, d), lambda i: (i, 0)), pl.BlockSpec((1, d), lambda i: (0, 0))],
        out_specs=pl.BlockSpec((tm, d), lambda i: (i, 0)),
        out_shape=jax.ShapeDtypeStruct((n, d), F32),
        compiler_params=_params("arbitrary"),
        name="final_norm",
    )(x2, w)


def _pad_lanes(v):
    return jnp.pad(v, (0, LANES - v.shape[0]))[None, :]


def _prepare(norm_mix, w_in, conv_w, conv_b, ssd_dt_bias, ssd_a_log, ssd_d, ssd_norm, ret_log2_decay,
             attn_q_norm, attn_k_norm, w_branch, w_out, norm_ffn, w_router, w_up, w_down, norm_final):
    p = dict(
        norm_mix=norm_mix[:, None, :],
        w_proj=_repack_proj_weights(w_in),
        conv_w=conv_w, conv_b=conv_b[:, None, :],
        dtb=[_pad_lanes(ssd_dt_bias[l].reshape(-1)) for l in range(DEPTH)],
        alog=[_pad_lanes(ssd_a_log[l].reshape(-1)) for l in range(DEPTH)],
        dsk=[jnp.repeat(ssd_d[l], SSD_HEAD_DIM)[None, :] for l in range(DEPTH)],
        ssd_norm=ssd_norm[:, None, :],
        log_gamma=jnp.log1p(-jnp.exp2(ret_log2_decay.astype(F32))).reshape(DEPTH, 2 * RET_HEADS),
        q_norm=attn_q_norm, k_norm=attn_k_norm,
        w_branch=w_branch.astype(BF16), w_out=w_out.astype(BF16),
        norm_ffn=norm_ffn[:, None, :],
        w_router=jnp.pad(w_router, ((0, 0), (0, 0), (0, LANES - N_EXPERTS))).astype(BF16),
        w_up=w_up.astype(BF16), w_down=w_down.astype(BF16),
        norm_final=norm_final[None, :],
    )
    qw = ATT_HEADS * ATT_HEAD_DIM
    hid = np.arange(qw) // ATT_HEAD_DIM
    p["ones_bd"] = jnp.asarray(hid[:, None] == hid[None, :], dtype=BF16)
    return p


def _trunk(x, p):
    bsz, seq, d = x.shape
    n = bsz * seq
    cap = (EC_CAPACITY_FACTOR * n) // N_EXPERTS
    ret_cos, ret_sin = _rotary_tables(seq, RET_DIM)
    att_cos, att_sin = [jnp.tile(t, (1, ATT_KV_HEADS)) for t in _rotary_tables(seq, ATT_HEAD_DIM)]
    fft_tables = _fft_tables(seq)
    x2 = x.reshape(n, d)
    for l in range(DEPTH):
        proj = _norm_proj(x2, p["norm_mix"], p["w_proj"], l)
        o_ssd = _ssd_mixer(proj, bsz, seq, p["conv_w"][l], p["conv_b"][l], p["dtb"][l], p["alog"][l], p["dsk"][l],
                           p["ssd_norm"][l])
        o_ret = _ret_mixer(proj, bsz, seq, p["log_gamma"][l], ret_cos, ret_sin)
        o_att = _att_mixer(proj, bsz, seq, p["q_norm"][l], p["k_norm"][l], att_cos, att_sin, p["ones_bd"])
        o_fft = _fft_mixer(proj, bsz, seq, fft_tables)
        u_copies = max(1, 32768 // n)
        x2, u, aff = _merge(x2, proj, (o_ssd, o_ret, o_att, o_fft), p["w_branch"], p["w_out"], p["norm_ffn"],
                            p["w_router"], l, u_copies)
        gate, idx = lax.top_k(aff[:, :N_EXPERTS].T, cap)
        idx, gate = lax.sort_key_val(idx, gate, dimension=1)
        xe = u.reshape(u_copies * n, d).at[idx.reshape(-1)].get(mode="promise_in_bounds").reshape(N_EXPERTS, cap, d)
        ye = _expert_ffn(xe, gate[..., None], p["w_up"], p["w_down"], l)
        x2 = _combine(x2, ye.reshape(-1, d), idx)
    return _final_norm(x2, p["norm_final"]).reshape(bsz, seq, d)


def kernel(x_prompt, x_sample, norm_mix, w_in, conv_w, conv_b, ssd_dt_bias, ssd_a_log, ssd_d, ssd_norm, ret_log2_decay, attn_q_norm, attn_k_norm, w_branch, w_out, norm_ffn, w_router, w_up, w_down, norm_final):
    p = _prepare(norm_mix, w_in, conv_w, conv_b, ssd_dt_bias, ssd_a_log, ssd_d, ssd_norm, ret_log2_decay,
                 attn_q_norm, attn_k_norm, w_branch, w_out, norm_ffn, w_router, w_up, w_down, norm_final)
    return (_trunk(x_prompt, p), _trunk(x_sample, p))
```
